```python
import jax, jax.numpy as jnp
from jax import lax
import numpy as np

D_MODEL = 2048
BATCH = 4
SEQ = 2048
DEPTH = 2
DEC_BATCH = 128
DEC_SEQ = 1
PAST_LEN = 16384
PAGE_SIZE = 128

N_MIXERS = 2
N_POOL = (DEPTH + 1) // 2
N_GLA = DEPTH // 2
EPS = 1e-6

POOL_WIDTH = D_MODEL // 2
POOL_GROUPS = 4
POOL_GC = POOL_WIDTH // POOL_GROUPS
POOL_WINDOWS = (2, 4, 8, 16)
POOL_BUF = max(POOL_WINDOWS) - 1

MEM_TOKENS = 256
MEM_HEADS = 4
MEM_HEAD_DIM = D_MODEL // 8
MEM_WIDTH = MEM_HEADS * MEM_HEAD_DIM
MEM_SCALE = MEM_HEAD_DIM ** -0.5

GLA_HEADS = 4
GLA_KEY = D_MODEL // 2
GLA_VAL = D_MODEL
GLA_DK = GLA_KEY // GLA_HEADS
GLA_DV = GLA_VAL // GLA_HEADS
GATE_RANK = 16
GATE_TAU = 16.0
GLA_CHUNK = 64
GLA_SCALE = GLA_DK ** -0.5
GLA_IN_MIX = 2 * GLA_KEY + GLA_VAL + GATE_RANK + GLA_VAL

N_KEYS = 128
N_EXPERTS = N_KEYS * N_KEYS
PEER_HEADS = 8
PEER_QDIM = 256
PEER_HALF = PEER_QDIM // 2
PEER_TOPK = 16
PEER_BLOCK = 128

kernel_name = "hybrid_pool_gla_peer_memxattn_step"


def rmsnorm(x, g):
    xf = x.astype(jnp.float32)
    y = xf * lax.rsqrt(jnp.mean(xf * xf, axis=-1, keepdims=True) + EPS) * g.astype(jnp.float32)
    return y.astype(x.dtype)


def mem_kv(mem, g, w):
    B, M, _ = mem.shape
    kv = rmsnorm(mem, g) @ w
    k = kv[..., :MEM_WIDTH].reshape(B, M, MEM_HEADS, MEM_HEAD_DIM)
    v = kv[..., MEM_WIDTH:].reshape(B, M, MEM_HEADS, MEM_HEAD_DIM)
    return k, v


def mem_attend(qm, mk, mv):
    B, T, _ = qm.shape
    q = qm.reshape(B, T, MEM_HEADS, MEM_HEAD_DIM).astype(jnp.float32)
    s = jnp.einsum('bthd,bmhd->bhtm', q, mk.astype(jnp.float32)) * MEM_SCALE
    p = jax.nn.softmax(s, axis=-1)
    o = jnp.einsum('bhtm,bmhd->bthd', p, mv.astype(jnp.float32))
    return o.reshape(B, T, MEM_WIDTH).astype(qm.dtype)


def pool_mixer(p, prev, start_pos, w_group, scale):
    B, T, _ = p.shape
    ext = jnp.concatenate([prev.astype(p.dtype), p], axis=1)
    cs = jnp.cumsum(ext.astype(jnp.float32), axis=1)
    cs = jnp.concatenate([jnp.zeros((B, 1, POOL_WIDTH), jnp.float32), cs], axis=1)
    cs = cs.reshape(B, POOL_BUF + T + 1, POOL_GROUPS, POOL_GC)
    pos = start_pos + jnp.arange(T)
    hi = POOL_BUF + 1
    means = []
    for g, w in enumerate(POOL_WINDOWS):
        s = cs[:, hi:hi + T, g] - cs[:, hi - w:hi - w + T, g]
        cnt = jnp.minimum(w, pos + 1).astype(jnp.float32)
        means.append(s / cnt[None, :, None])
    mean = jnp.stack(means, axis=2)
    d = mean - p.reshape(B, T, POOL_GROUPS, POOL_GC).astype(jnp.float32)
    out = jnp.einsum('btgc,gce->btge', d, w_group.astype(jnp.float32))
    out = out * scale.astype(jnp.float32).reshape(POOL_GROUPS, POOL_GC)
    return out.reshape(B, T, POOL_WIDTH).astype(p.dtype), ext[:, -POOL_BUF:]


def gla_chunked(q, k, v, log_a, S0):
    B, T, H, DK = q.shape
    DV = v.shape[-1]
    C = min(GLA_CHUNK, T)
    n = -(-T // C)
    pad = n * C - T

    def blocks(a):
        a = jnp.pad(a.astype(jnp.float32), ((0, 0), (0, pad), (0, 0), (0, 0)))
        return a.reshape(B, n, C, H, a.shape[-1]).transpose(1, 0, 2, 3, 4)

    q, k, v, la = blocks(q), blocks(k), blocks(v), blocks(log_a)
    b = jnp.cumsum(la, axis=2)
    q_in = q * jnp.exp(b) * GLA_SCALE
    k_in = k * jnp.exp(-b)
    k_out = k * jnp.exp(b[:, :, -1:] - b)
    decay = jnp.exp(b[:, :, -1])
    causal = jnp.tril(jnp.ones((C, C), dtype=bool))
    A = jnp.where(causal, jnp.einsum('nbchk,nbshk->nbhcs', q_in, k_in), 0.0)
    o_intra = jnp.einsum('nbhcs,nbshv->nbchv', A, v)

    def step(S, xs):
        qc, kc, vc, dc = xs
        o = jnp.einsum('bchk,bhkv->bchv', qc, S)
        S = S * dc[..., None] + jnp.einsum('bchk,bchv->bhkv', kc, vc)
        return S, o

    S, o_inter = lax.scan(step, S0.astype(jnp.float32), (q_in, k_out, v, decay))
    o = (o_intra + o_inter).transpose(1, 0, 2, 3, 4).reshape(B, n * C, H, DV)[:, :T]
    return o, S


def gla_mixer(u, S0, w_gate, b_gate, norm_g):
    B, T, _ = u.shape
    o1 = GLA_KEY
    o2 = o1 + GLA_KEY
    o3 = o2 + GLA_VAL
    o4 = o3 + GATE_RANK
    q = u[..., :o1].reshape(B, T, GLA_HEADS, GLA_DK)
    k = u[..., o1:o2].reshape(B, T, GLA_HEADS, GLA_DK)
    v = u[..., o2:o3].reshape(B, T, GLA_HEADS, GLA_DV)
    z = u[..., o3:o4]
    og = u[..., o4:].reshape(B, T, GLA_HEADS, GLA_DV)
    log_a = jax.nn.log_sigmoid((z @ w_gate + b_gate).astype(jnp.float32)) / GATE_TAU
    log_a = log_a.reshape(B, T, GLA_HEADS, GLA_DK)
    o, S = gla_chunked(q, k, v, log_a, S0)
    o = rmsnorm(o, norm_g) * jax.nn.silu(og.astype(jnp.float32))
    return o.reshape(B, T, GLA_VAL).astype(u.dtype), S


def peer(h, w_q, keys, u_tab, v_tab):
    B, T, D = h.shape
    n = B * T
    nb = -(-n // PEER_BLOCK)
    xf = jnp.pad(h.reshape(n, D), ((0, nb * PEER_BLOCK - n), (0, 0)))
    xb = xf.reshape(nb, PEER_BLOCK, D)
    kf = keys.astype(jnp.float32)

    def block(xt):
        q = (xt @ w_q).reshape(PEER_BLOCK, PEER_HEADS, 2, PEER_HALF).astype(jnp.float32)
        s = jnp.einsum('nhpc,hpkc->nhpk', q, kf)
        sv, si = lax.top_k(s, PEER_TOPK)
        cand = sv[:, :, 0, :, None] + sv[:, :, 1, None, :]
        cid = si[:, :, 0, :, None] * N_KEYS + si[:, :, 1, None, :]
        cs, ci = lax.top_k(cand.reshape(PEER_BLOCK, PEER_HEADS, PEER_TOPK * PEER_TOPK), PEER_TOPK)
        eid = jnp.take_along_axis(cid.reshape(PEER_BLOCK, PEER_HEADS, PEER_TOPK * PEER_TOPK), ci, axis=-1)
        g = jax.nn.softmax(cs, axis=-1)
        a = jnp.einsum('nd,nhkd->nhk', xt, u_tab[eid]).astype(jnp.float32)
        hh = (jax.nn.gelu(a, approximate=False) * g).astype(xt.dtype)
        return jnp.einsum('nhk,nhkd->nd', hh, v_tab[eid])

    out = lax.map(block, xb).reshape(nb * PEER_BLOCK, D)[:n]
    return out.reshape(B, T, D)


def trunk(x, mem_k, mem_v, pool_init, gla_init, start_pos, p):
    pool_new, gla_new = [], []
    for i in range(DEPTH):
        h = rmsnorm(x, p['norm_mix'][i])
        j = i // N_MIXERS
        if i % N_MIXERS == 0:
            u = h @ p['pool_w_in'][j]
            mix, st = pool_mixer(u[..., :POOL_WIDTH], pool_init[j], start_pos,
                                 p['pool_w_group'][j], p['pool_scale'][j])
            qm = u[..., POOL_WIDTH:]
            w_out = p['pool_w_out'][j]
            pool_new.append(st)
        else:
            u = h @ p['gla_w_in'][j]
            mix, st = gla_mixer(u[..., :GLA_IN_MIX], gla_init[j], p['gla_w_gate'][j],
                                p['gla_b_gate'][j], p['gla_norm'][j])
            qm = u[..., GLA_IN_MIX:]
            w_out = p['gla_w_out'][j]
            gla_new.append(st)
        att = mem_attend(qm, mem_k[i], mem_v[i])
        x = x + jnp.concatenate([mix, att], axis=-1) @ w_out
        x = x + peer(rmsnorm(x, p['norm_ffn'][i]), p['peer_w_q'][i], p['peer_keys'][i],
                     p['peer_u'][i], p['peer_v'][i])
    return rmsnorm(x, p['norm_final']), jnp.stack(pool_new), jnp.stack(gla_new)


def setup_inputs(seed: int = 0) -> dict:
    key = jax.random.key(seed)
    ks = iter(jax.random.split(key, 40))

    def nrm(shape, scale=1.0):
        return jax.random.normal(next(ks), shape, jnp.float32) * scale

    def gain(shape):
        return 1.0 + 0.02 * nrm(shape)

    D = D_MODEL
    return {
        "x_prompt": nrm((BATCH, SEQ, D)),
        "x_sample": nrm((DEC_BATCH, DEC_SEQ, D)),
        "cache_mem_k": nrm((DEPTH, DEC_BATCH, MEM_TOKENS, MEM_HEADS, MEM_HEAD_DIM)),
        "cache_mem_v": nrm((DEPTH, DEC_BATCH, MEM_TOKENS, MEM_HEADS, MEM_HEAD_DIM)),
        "state_pool": nrm((N_POOL, DEC_BATCH, POOL_BUF, POOL_WIDTH)),
        "state_gla": nrm((N_GLA, DEC_BATCH, GLA_HEADS, GLA_DK, GLA_DV), 0.3),
        "mem_prompt": nrm((BATCH, MEM_TOKENS, D)),
        "norm_mix": gain((DEPTH, D)),
        "norm_ffn": gain((DEPTH, D)),
        "norm_final": gain((D,)),
        "mem_norm": gain((DEPTH, D)),
        "w_mem_kv": nrm((DEPTH, D, 2 * MEM_WIDTH), D ** -0.5),
        "pool_w_in": nrm((N_POOL, D, POOL_WIDTH + MEM_WIDTH), D ** -0.5),
        "pool_w_group": nrm((N_POOL, POOL_GROUPS, POOL_GC, POOL_GC), POOL_GC ** -0.5),
        "pool_scale": gain((N_POOL, POOL_WIDTH)),
        "pool_w_out": nrm((N_POOL, POOL_WIDTH + MEM_WIDTH, D), (POOL_WIDTH + MEM_WIDTH) ** -0.5),
        "gla_w_in": nrm((N_GLA, D, GLA_IN_MIX + MEM_WIDTH), D ** -0.5),
        "gla_w_gate": nrm((N_GLA, GATE_RANK, GLA_KEY), GATE_RANK ** -0.5),
        "gla_b_gate": nrm((N_GLA, GLA_KEY), 0.01),
        "gla_norm": gain((N_GLA, GLA_DV)),
        "gla_w_out": nrm((N_GLA, GLA_VAL + MEM_WIDTH, D), (GLA_VAL + MEM_WIDTH) ** -0.5),
        "peer_w_q": nrm((DEPTH, D, PEER_HEADS * PEER_QDIM), D ** -0.5),
        "peer_keys": nrm((DEPTH, PEER_HEADS, 2, N_KEYS, PEER_HALF), PEER_HALF ** -0.5),
        "peer_u": nrm((DEPTH, N_EXPERTS, D), D ** -0.5),
        "peer_v": nrm((DEPTH, N_EXPERTS, D), PEER_HEADS ** -0.5),
    }


def reference(x_prompt, x_sample, cache_mem_k, cache_mem_v, state_pool, state_gla, mem_prompt,
              norm_mix, norm_ffn, norm_final, mem_norm, w_mem_kv,
              pool_w_in, pool_w_group, pool_scale, pool_w_out,
              gla_w_in, gla_w_gate, gla_b_gate, gla_norm, gla_w_out,
              peer_w_q, peer_keys, peer_u, peer_v):
    p = dict(norm_mix=norm_mix, norm_ffn=norm_ffn, norm_final=norm_final,
             pool_w_in=pool_w_in, pool_w_group=pool_w_group, pool_scale=pool_scale,
             pool_w_out=pool_w_out, gla_w_in=gla_w_in, gla_w_gate=gla_w_gate,
             gla_b_gate=gla_b_gate, gla_norm=gla_norm, gla_w_out=gla_w_out,
             peer_w_q=peer_w_q, peer_keys=peer_keys, peer_u=peer_u, peer_v=peer_v)

    mks, mvs = [], []
    for i in range(DEPTH):
        mk, mv = mem_kv(mem_prompt, mem_norm[i], w_mem_kv[i])
        mks.append(mk)
        mvs.append(mv)
    mem_k_prompt = jnp.stack(mks)
    mem_v_prompt = jnp.stack(mvs)
    B = x_prompt.shape[0]
    pool0 = jnp.zeros((N_POOL, B, POOL_BUF, POOL_WIDTH), x_prompt.dtype)
    gla0 = jnp.zeros((N_GLA, B, GLA_HEADS, GLA_DK, GLA_DV), jnp.float32)
    y_prompt, pool_prompt, gla_prompt = trunk(x_prompt, mem_k_prompt, mem_v_prompt,
                                              pool0, gla0, 0, p)

    y_sample, pool_sample, gla_sample = trunk(x_sample, cache_mem_k, cache_mem_v,
                                              state_pool, state_gla, PAST_LEN, p)

    return (y_prompt, y_sample, pool_prompt, gla_prompt, mem_k_prompt, mem_v_prompt,
            pool_sample, gla_sample)
```

```python
import functools

import jax
import jax.numpy as jnp
from jax import lax
from jax.experimental import pallas as pl
from jax.experimental.pallas import tpu as pltpu

D_MODEL = 2048
BATCH = 4
SEQ = 2048
DEPTH = 2
DEC_BATCH = 128
PAST_LEN = 16384
EPS = 1e-6

N_PROMPT = BATCH * SEQ
N_TOK = N_PROMPT + DEC_BATCH

POOL_WIDTH = D_MODEL // 2
POOL_GROUPS = 4
POOL_GC = POOL_WIDTH // POOL_GROUPS
POOL_WINDOWS = (2, 4, 8, 16)
POOL_BUF = max(POOL_WINDOWS) - 1

MEM_TOKENS = 256
MEM_HEADS = 4
MEM_HEAD_DIM = D_MODEL // 8
MEM_WIDTH = MEM_HEADS * MEM_HEAD_DIM
MEM_SCALE = MEM_HEAD_DIM ** -0.5

GLA_HEADS = 4
GLA_KEY = D_MODEL // 2
GLA_VAL = D_MODEL
GLA_DK = GLA_KEY // GLA_HEADS
GLA_DV = GLA_VAL // GLA_HEADS
GATE_RANK = 16
GATE_TAU = 16.0
GLA_CHUNK = 64
GLA_SCALE = GLA_DK ** -0.5

N_KEYS = 128
N_EXPERTS = N_KEYS * N_KEYS
PEER_HEADS = 8
PEER_HALF = 128
PEER_TOPK = 16

LANES = 128
SUBLANES = 8
VMEM_LIMIT = 56 * 1024 * 1024

GLA_COL_Q = 0
GLA_COL_K = GLA_KEY
GLA_COL_V = 2 * GLA_KEY
GLA_COL_OG = 2 * GLA_KEY + GLA_VAL
GLA_COL_QM = 2 * GLA_KEY + 2 * GLA_VAL
GLA_COL_Z = GLA_COL_QM + MEM_WIDTH
GLA_Z_PAD = 512
GLA_IN_COLS = GLA_COL_Z + GLA_Z_PAD

BF16 = jnp.bfloat16
F32 = jnp.float32

_NT = (((1,), (1,)), ((), ()))
_TN = (((0,), (0,)), ((), ()))


def _params(*sem):
    return pltpu.CompilerParams(dimension_semantics=sem, vmem_limit_bytes=VMEM_LIMIT)


def _rmsnorm_kernel(x_ref, g_ref, *o_refs):
    x = x_ref[...]
    y = x * lax.rsqrt(jnp.mean(x * x, axis=-1, keepdims=True) + EPS) * g_ref[...]
    o_refs[0][...] = y.astype(o_refs[0].dtype)
    if len(o_refs) > 1:
        o_refs[1][...] = jnp.transpose(y).astype(o_refs[1].dtype)


def rmsnorm(x, g, out_dtype, *, tm, row0=0, rows=None, transposed=False):
    n, d = x.shape
    rows = n - row0 if rows is None else rows
    assert rows % tm == 0 and row0 % tm == 0
    blk0 = row0 // tm
    out_shape = [jax.ShapeDtypeStruct((rows, d), out_dtype)]
    out_specs = [pl.BlockSpec((tm, d), lambda i: (i, 0))]
    if transposed:
        out_shape.append(jax.ShapeDtypeStruct((d, rows), out_dtype))
        out_specs.append(pl.BlockSpec((d, tm), lambda i: (0, i)))
    res = pl.pallas_call(
        _rmsnorm_kernel,
        grid=(rows // tm,),
        in_specs=[pl.BlockSpec((tm, d), lambda i: (i + blk0, 0)),
                  pl.BlockSpec((1, d), lambda i: (0, 0))],
        out_specs=out_specs,
        out_shape=out_shape,
        compiler_params=_params("parallel"),
        name="rmsnorm",
    )(x, g.reshape(1, d))
    return res if transposed else res[0]


def _matmul_kernel(*refs, n_pairs, has_res):
    o_ref = refs[-1]
    acc = None
    for a_ref, w_ref in zip(refs[:n_pairs], refs[n_pairs:2 * n_pairs]):
        p = jnp.dot(a_ref[...], w_ref[...], preferred_element_type=F32)
        acc = p if acc is None else acc + p
    if has_res:
        acc = acc + refs[2 * n_pairs][...]
    o_ref[...] = acc.astype(o_ref.dtype)


def matmul(a_list, w_list, residual=None, *, tm, tn):
    n = a_list[0].shape[0]
    f = w_list[0].shape[1]
    assert n % tm == 0 and f % tn == 0
    in_specs = [pl.BlockSpec((tm, a.shape[1]), lambda i, j: (i, 0)) for a in a_list]
    in_specs += [pl.BlockSpec((w.shape[0], tn), lambda i, j: (0, j)) for w in w_list]
    args = list(a_list) + list(w_list)
    if residual is not None:
        in_specs.append(pl.BlockSpec((tm, tn), lambda i, j: (i, j)))
        args.append(residual)
    return pl.pallas_call(
        functools.partial(_matmul_kernel, n_pairs=len(a_list), has_res=residual is not None),
        grid=(n // tm, f // tn),
        in_specs=in_specs,
        out_specs=pl.BlockSpec((tm, tn), lambda i, j: (i, j)),
        out_shape=jax.ShapeDtypeStruct((n, f), F32),
        compiler_params=_params("parallel", "parallel"),
        name="matmul",
    )(*args)


def _pool_prompt_kernel(p_ref, wg_ref, sc_ref, o_ref):
    g = pl.program_id(1)
    x = p_ref[...]
    t_len = x.shape[0]
    row = lax.broadcasted_iota(jnp.int32, (t_len, 1), 0)

    def shifted(v, k):
        return jnp.where(row >= k, pltpu.roll(v, k, 0), 0.0)

    for gi, w in enumerate(POOL_WINDOWS):
        @pl.when(g == gi)
        def _(w=w):
            s = x
            k = 1
            while k < w:
                s = s + shifted(s, k)
                k *= 2
            cnt = jnp.minimum(w, row + 1).astype(F32)
            d = s / cnt - x
            out = jnp.dot(d.astype(BF16), wg_ref[0], preferred_element_type=F32)
            o_ref[...] = (out * sc_ref[0]).astype(o_ref.dtype)


def pool_prompt(u, w_group, scale):
    return pl.pallas_call(
        _pool_prompt_kernel,
        grid=(BATCH, POOL_GROUPS),
        in_specs=[pl.BlockSpec((SEQ, POOL_GC), lambda b, g: (b, g)),
                  pl.BlockSpec((1, POOL_GC, POOL_GC), lambda b, g: (g, 0, 0)),
                  pl.BlockSpec((1, 1, POOL_GC), lambda b, g: (g, 0, 0))],
        out_specs=pl.BlockSpec((SEQ, POOL_GC), lambda b, g: (b, g)),
        out_shape=jax.ShapeDtypeStruct((N_PROMPT, POOL_WIDTH), BF16),
        compiler_params=_params("parallel", "parallel"),
        name="pool_prompt",
    )(u, w_group.astype(BF16), scale.reshape(POOL_GROUPS, 1, POOL_GC))


def _pool_sample_kernel(prev_ref, p_ref, wg_ref, sc_ref, o_ref):
    g = pl.program_id(0)
    x = p_ref[...]
    for gi, w in enumerate(POOL_WINDOWS):
        @pl.when(g == gi)
        def _(w=w):
            s = x
            for r in range(POOL_BUF - (w - 1), POOL_BUF):
                s = s + prev_ref[r]
            cnt = float(min(w, PAST_LEN + 1))
            d = s / cnt - x
            out = jnp.dot(d.astype(BF16), wg_ref[0], preferred_element_type=F32)
            o_ref[...] = (out * sc_ref[0]).astype(o_ref.dtype)


def pool_sample(prev_t, p, w_group, scale):
    b = p.shape[0]
    return pl.pallas_call(
        _pool_sample_kernel,
        grid=(POOL_GROUPS,),
        in_specs=[pl.BlockSpec((POOL_BUF, b, POOL_GC), lambda g: (0, 0, g)),
                  pl.BlockSpec((b, POOL_GC), lambda g: (0, g)),
                  pl.BlockSpec((1, POOL_GC, POOL_GC), lambda g: (g, 0, 0)),
                  pl.BlockSpec((1, 1, POOL_GC), lambda g: (g, 0, 0))],
        out_specs=pl.BlockSpec((b, POOL_GC), lambda g: (0, g)),
        out_shape=jax.ShapeDtypeStruct((b, POOL_WIDTH), BF16),
        compiler_params=_params("parallel"),
        name="pool_sample",
    )(prev_t, p, w_group.astype(BF16), scale.reshape(POOL_GROUPS, 1, POOL_GC))


def _attn_prompt_kernel(q_ref, kv_ref, o_ref):
    for h in range(MEM_HEADS):
        lo, hi = h * MEM_HEAD_DIM, (h + 1) * MEM_HEAD_DIM
        qh = q_ref[:, lo:hi].astype(BF16)
        kh = kv_ref[:, lo:hi].astype(BF16)
        vh = kv_ref[:, MEM_WIDTH + lo:MEM_WIDTH + hi].astype(BF16)
        s = lax.dot_general(qh, kh, _NT, preferred_element_type=F32) * MEM_SCALE
        e = jnp.exp(s - jnp.max(s, axis=-1, keepdims=True))
        l = jnp.sum(e, axis=-1, keepdims=True)
        o = jnp.dot(e.astype(BF16), vh, preferred_element_type=F32) / l
        o_ref[:, lo:hi] = o.astype(o_ref.dtype)


def attn_prompt(u, q_col, kv, *, tq=512):
    assert q_col % MEM_WIDTH == 0
    nt = SEQ // tq
    return pl.pallas_call(
        _attn_prompt_kernel,
        grid=(BATCH, nt),
        in_specs=[pl.BlockSpec((tq, MEM_WIDTH), lambda b, t: (b * nt + t, q_col // MEM_WIDTH)),
                  pl.BlockSpec((MEM_TOKENS, 2 * MEM_WIDTH), lambda b, t: (b, 0))],
        out_specs=pl.BlockSpec((tq, MEM_WIDTH), lambda b, t: (b * nt + t, 0)),
        out_shape=jax.ShapeDtypeStruct((N_PROMPT, MEM_WIDTH), BF16),
        compiler_params=_params("parallel", "parallel"),
        name="attn_prompt",
    )(u, kv)


def _attn_sample_kernel(q_ref, k_ref, v_ref, o_ref):
    for i in range(q_ref.shape[0]):
        prod = k_ref[i] * q_ref[i]
        vb = v_ref[i]
        for h in range(MEM_HEADS):
            lo, hi = h * MEM_HEAD_DIM, (h + 1) * MEM_HEAD_DIM
            s = jnp.sum(prod[:, lo:hi], axis=-1, keepdims=True) * MEM_SCALE
            e = jnp.exp(s - jnp.max(s, axis=0, keepdims=True))
            p = e / jnp.sum(e, axis=0, keepdims=True)
            o = jnp.sum(p * vb[:, lo:hi], axis=0, keepdims=True)
            o_ref[i, :, lo:hi] = o.astype(o_ref.dtype)


def attn_sample(q, mem_k, mem_v, *, bb=4):
    b = q.shape[0]
    return pl.pallas_call(
        _attn_sample_kernel,
        grid=(b // bb,),
        in_specs=[pl.BlockSpec((bb, 1, MEM_WIDTH), lambda i: (i, 0, 0)),
                  pl.BlockSpec((bb, MEM_TOKENS, MEM_WIDTH), lambda i: (i, 0, 0)),
                  pl.BlockSpec((bb, MEM_TOKENS, MEM_WIDTH), lambda i: (i, 0, 0))],
        out_specs=pl.BlockSpec((bb, 1, MEM_WIDTH), lambda i: (i, 0, 0)),
        out_shape=jax.ShapeDtypeStruct((b, 1, MEM_WIDTH), BF16),
        compiler_params=_params("parallel"),
        name="attn_sample",
    )(q, mem_k, mem_v)


def _log_decay(z, wg_ref, bg_ref):
    g = jnp.dot(z.astype(BF16), wg_ref[...], preferred_element_type=F32) + bg_ref[...]
    return jax.nn.log_sigmoid(g) / GATE_TAU


def _gla_out(o, og, gn):
    on = o * lax.rsqrt(jnp.mean(o * o, axis=-1, keepdims=True) + EPS) * gn
    return on * (og * jax.nn.sigmoid(og))


def _gla_prompt_kernel(q_ref, k_ref, v_ref, og_ref, z_ref, wg_ref, bg_ref, gn_ref,
                       mix_ref, st_ref, s_ref):
    n = pl.program_id(1)
    c = q_ref.shape[0]

    @pl.when(n == 0)
    def _():
        s_ref[...] = jnp.zeros_like(s_ref)

    la = _log_decay(z_ref[...], wg_ref, bg_ref)
    row = lax.broadcasted_iota(jnp.int32, (c, 1), 0)
    b = la
    k = 1
    while k < c:
        b = b + jnp.where(row >= k, pltpu.roll(b, k, 0), 0.0)
        k *= 2
    b_last = b[c - 1:c, :]
    q_in = q_ref[...] * jnp.exp(b) * GLA_SCALE
    k_in = k_ref[...] * jnp.exp(-b)
    k_out = k_ref[...] * jnp.exp(b_last - b)
    decay = jnp.exp(b_last)
    causal = row >= lax.broadcasted_iota(jnp.int32, (1, c), 1)
    gn = gn_ref[...]
    for h in range(GLA_HEADS):
        klo, khi = h * GLA_DK, (h + 1) * GLA_DK
        vlo, vhi = h * GLA_DV, (h + 1) * GLA_DV
        qh = q_in[:, klo:khi].astype(BF16)
        vh = v_ref[:, vlo:vhi].astype(BF16)
        a = lax.dot_general(qh, k_in[:, klo:khi].astype(BF16), _NT, preferred_element_type=F32)
        a = jnp.where(causal, a, 0.0)
        st = s_ref[h]
        o = (jnp.dot(a.astype(BF16), vh, preferred_element_type=F32)
             + lax.dot_general(qh, st.astype(BF16), _NT, preferred_element_type=F32))
        s_ref[h] = st * decay[:, klo:khi] + lax.dot_general(
            vh, k_out[:, klo:khi].astype(BF16), _TN, preferred_element_type=F32)
        mix_ref[:, vlo:vhi] = _gla_out(o, og_ref[:, vlo:vhi], gn).astype(mix_ref.dtype)

    @pl.when(n == pl.num_programs(1) - 1)
    def _():
        st_ref[0] = s_ref[...]


def gla_prompt(u, wg_pad, b_gate, norm_g):
    nc = SEQ // GLA_CHUNK
    c = GLA_CHUNK

    def col(width, off):
        assert off % width == 0
        return pl.BlockSpec((c, width), lambda b, n: (b * nc + n, off // width))

    return pl.pallas_call(
        _gla_prompt_kernel,
        grid=(BATCH, nc),
        in_specs=[col(GLA_KEY, GLA_COL_Q), col(GLA_KEY, GLA_COL_K), col(GLA_VAL, GLA_COL_V),
                  col(GLA_VAL, GLA_COL_OG), col(LANES, GLA_COL_Z),
                  pl.BlockSpec((LANES, GLA_KEY), lambda b, n: (0, 0)),
                  pl.BlockSpec((1, GLA_KEY), lambda b, n: (0, 0)),
                  pl.BlockSpec((1, GLA_DV), lambda b, n: (0, 0))],
        out_specs=[pl.BlockSpec((c, GLA_VAL), lambda b, n: (b * nc + n, 0)),
                   pl.BlockSpec((1, GLA_HEADS, GLA_DV, GLA_DK), lambda b, n: (b, 0, 0, 0))],
        out_shape=[jax.ShapeDtypeStruct((N_PROMPT, GLA_VAL), BF16),
                   jax.ShapeDtypeStruct((BATCH, GLA_HEADS, GLA_DV, GLA_DK), F32)],
        scratch_shapes=[pltpu.VMEM((GLA_HEADS, GLA_DV, GLA_DK), F32)],
        compiler_params=_params("parallel", "arbitrary"),
        name="gla_prompt",
    )(u, u, u, u, u, wg_pad, b_gate.reshape(1, GLA_KEY), norm_g.reshape(1, GLA_DV))


def _gla_sample_kernel(us_ref, s0_ref, wg_ref, bg_ref, gn_ref, mix_ref, s1_ref, la_ref):
    i = pl.program_id(0)

    @pl.when(i == 0)
    def _():
        la_ref[...] = _log_decay(us_ref[:, GLA_COL_Z:GLA_COL_Z + LANES], wg_ref, bg_ref)

    tok = pl.ds(i, 1)
    gn = gn_ref[...]
    for h in range(GLA_HEADS):
        klo, khi = h * GLA_DK, (h + 1) * GLA_DK
        vlo, vhi = h * GLA_DV, (h + 1) * GLA_DV
        la = la_ref[tok, klo:khi]
        q = us_ref[tok, GLA_COL_Q + klo:GLA_COL_Q + khi]
        k = us_ref[tok, GLA_COL_K + klo:GLA_COL_K + khi]
        v = us_ref[tok, GLA_COL_V + vlo:GLA_COL_V + vhi]
        og = us_ref[tok, GLA_COL_OG + vlo:GLA_COL_OG + vhi]
        q_in = q * jnp.exp(la) * GLA_SCALE
        k_in = k * jnp.exp(-la)
        k_out = k * jnp.exp(la - la)
        decay = jnp.exp(la)
        a = jnp.sum(q_in * k_in, axis=-1, keepdims=True)
        rows = jnp.concatenate(
            [q_in, k_out, decay, jnp.zeros((SUBLANES - 3, GLA_DK), F32)], axis=0)
        cols = jnp.transpose(rows)
        s0 = s0_ref[0, h]
        o = a * v + jnp.sum(cols[:, 0:1] * s0, axis=0, keepdims=True)
        s1_ref[0, h] = s0 * cols[:, 2:3] + cols[:, 1:2] * v
        mix_ref[tok, vlo:vhi] = _gla_out(o, og, gn).astype(mix_ref.dtype)


def gla_sample(us, s0, wg_pad, b_gate, norm_g):
    b = us.shape[0]
    return pl.pallas_call(
        _gla_sample_kernel,
        grid=(b,),
        in_specs=[pl.BlockSpec((b, GLA_IN_COLS), lambda i: (0, 0)),
                  pl.BlockSpec((1, GLA_HEADS, GLA_DK, GLA_DV), lambda i: (i, 0, 0, 0)),
                  pl.BlockSpec((LANES, GLA_KEY), lambda i: (0, 0)),
                  pl.BlockSpec((1, GLA_KEY), lambda i: (0, 0)),
                  pl.BlockSpec((1, GLA_DV), lambda i: (0, 0))],
        out_specs=[pl.BlockSpec((b, GLA_VAL), lambda i: (0, 0)),
                   pl.BlockSpec((1, GLA_HEADS, GLA_DK, GLA_DV), lambda i: (i, 0, 0, 0))],
        out_shape=[jax.ShapeDtypeStruct((b, GLA_VAL), F32),
                   jax.ShapeDtypeStruct(s0.shape, F32)],
        scratch_shapes=[pltpu.VMEM((b, GLA_KEY), F32)],
        compiler_params=_params("arbitrary"),
        name="gla_sample",
    )(us, s0, wg_pad, b_gate.reshape(1, GLA_KEY), norm_g.reshape(1, GLA_DV))


def _peer_score_kernel(q_ref, keys_ref, o_ref):
    for hp in range(2 * PEER_HEADS):
        qs = q_ref[:, hp * PEER_HALF:(hp + 1) * PEER_HALF].astype(BF16)
        o_ref[hp] = lax.dot_general(keys_ref[hp], qs, _NT, preferred_element_type=F32)


def peer_scores(q, keys_bf16, *, tn=640):
    n = q.shape[0]
    assert n % tn == 0
    return pl.pallas_call(
        _peer_score_kernel,
        grid=(n // tn,),
        in_specs=[pl.BlockSpec((tn, q.shape[1]), lambda i: (i, 0)),
                  pl.BlockSpec((2 * PEER_HEADS, N_KEYS, PEER_HALF), lambda i: (0, 0, 0))],
        out_specs=pl.BlockSpec((2 * PEER_HEADS, N_KEYS, tn), lambda i: (0, 0, i)),
        out_shape=jax.ShapeDtypeStruct((2 * PEER_HEADS, N_KEYS, n), F32),
        compiler_params=_params("parallel"),
        name="peer_scores",
    )(q, keys_bf16)


_PEER_CAND = tuple((a, b) for a in range(PEER_TOPK) for b in range(PEER_TOPK)
                   if (a + 1) * (b + 1) <= PEER_TOPK)
_NEG_INF = float("-inf")
_UNRANKED = float(N_KEYS)


def _peer_select_kernel(s_ref, e0_ref, li_ref, e1_ref, rb_ref,
                        cur_ref, rank_ref, sv_ref, cand_ref, cnt_ref):
    tile = s_ref.shape[2:]

    for p in range(2):
        cur_ref[...] = s_ref[p]
        rank_ref[p] = jnp.full((N_KEYS,) + tile, _UNRANKED, F32)

        def extract(r, carry, p=p):
            m = lax.fori_loop(0, N_KEYS, lambda k, m: jnp.maximum(m, cur_ref[k]),
                              jnp.full(tile, _NEG_INF, F32))
            idx = lax.fori_loop(
                0, N_KEYS,
                lambda k, ix: jnp.minimum(ix, jnp.where(cur_ref[k] == m, k.astype(F32), _UNRANKED)),
                jnp.full(tile, _UNRANKED, F32))
            sv_ref[p, pl.ds(r, 1)] = m[None]
            rf = r.astype(F32)

            def knock(k, c):
                hit = idx == k.astype(F32)
                cur_ref[k] = jnp.where(hit, _NEG_INF, cur_ref[k])
                rank_ref[p, k] = jnp.where(hit, rf, rank_ref[p, k])
                return c

            return lax.fori_loop(0, N_KEYS, knock, carry)

        lax.fori_loop(0, PEER_TOPK, extract, 0)

    for ci, (a, b) in enumerate(_PEER_CAND):
        cand_ref[ci] = sv_ref[0, a] + sv_ref[1, b]
    cnt_ref[...] = jnp.zeros_like(cnt_ref)
    c_max = sv_ref[0, 0] + sv_ref[1, 0]
    no_flat = float(PEER_TOPK * PEER_TOPK)

    def pick(_, z):
        m = jnp.full(tile, _NEG_INF, F32)
        for ci in range(len(_PEER_CAND)):
            m = jnp.maximum(m, cand_ref[ci])
        flat = jnp.full(tile, no_flat, F32)
        for ci, (a, b) in enumerate(_PEER_CAND):
            flat = jnp.minimum(
                flat, jnp.where(cand_ref[ci] == m, float(a * PEER_TOPK + b), no_flat))
        for ci, (a, b) in enumerate(_PEER_CAND):
            cand_ref[ci] = jnp.where(flat == float(a * PEER_TOPK + b), _NEG_INF, cand_ref[ci])
        first = jnp.floor(flat * (1.0 / PEER_TOPK))
        for a in range(PEER_TOPK):
            cnt_ref[a] = cnt_ref[a] + jnp.where(first == float(a), 1.0, 0.0)
        return z + jnp.exp(m - c_max)

    inv_z = 1.0 / lax.fori_loop(0, PEER_TOPK, pick, jnp.zeros(tile, F32))

    max0 = sv_ref[0, 0]
    max1 = sv_ref[1, 0]

    def emit(k, carry):
        r0 = rank_ref[0, k]
        li = jnp.zeros(tile, F32)
        for a in range(PEER_TOPK):
            li = li + jnp.where(r0 == float(a), cnt_ref[a], 0.0)
        li_ref[0, k] = li
        e0_ref[0, k] = jnp.exp(s_ref[0, k] - max0) * inv_z
        e1_ref[0, k] = jnp.exp(s_ref[1, k] - max1)
        rb_ref[0, k] = rank_ref[1, k]
        return carry

    lax.fori_loop(0, N_KEYS, emit, 0)


def peer_select(s4, *, rows, rblk):
    assert rows % rblk == 0 and (rblk % SUBLANES == 0 or rblk == s4.shape[2])
    tile = (rblk, LANES)
    out = jax.ShapeDtypeStruct((PEER_HEADS, N_KEYS, rows, LANES), F32)
    out_spec = pl.BlockSpec((1, N_KEYS) + tile, lambda h, r: (h, 0, r, 0))
    res = pl.pallas_call(
        _peer_select_kernel,
        grid=(PEER_HEADS, rows // rblk),
        in_specs=[pl.BlockSpec((2, N_KEYS) + tile, lambda h, r: (h, 0, r, 0))],
        out_specs=[out_spec] * 4,
        out_shape=[out] * 4,
        scratch_shapes=[pltpu.VMEM((N_KEYS,) + tile, F32),
                        pltpu.VMEM((2, N_KEYS) + tile, F32),
                        pltpu.VMEM((2, PEER_TOPK) + tile, F32),
                        pltpu.VMEM((len(_PEER_CAND),) + tile, F32),
                        pltpu.VMEM((PEER_TOPK,) + tile, F32)],
        compiler_params=_params("parallel", "parallel"),
        name="peer_select",
    )(s4)
    return [r.reshape(PEER_HEADS, N_KEYS, rows * LANES) for r in res]


def _gelu(x):
    return 0.5 * x * (1.0 + lax.erf(x * (2.0 ** -0.5)))


def _peer_dense_kernel(ht_ref, u_ref, vt_ref, e0_ref, li_ref, e1_ref, rb_ref, o_ref,
                       a_ref, w_ref, *, ib):
    eb = pl.program_id(1)
    tn = ht_ref.shape[1]

    a_ref[...] = jnp.dot(u_ref[...], ht_ref[...], preferred_element_type=F32)

    n_chunks = tn // LANES
    half = N_KEYS // 2

    def chunk(t, carry, ii):
        jh = t // n_chunks
        cc = t % n_chunks
        lanes = pl.ds(pl.multiple_of(cc * LANES, LANES), LANES)
        js = pl.ds(pl.multiple_of(jh * half, half), half)
        rows = pl.ds(pl.multiple_of(ii * N_KEYS + jh * half, half), half)
        g = jnp.zeros((half, LANES), F32)
        for h in range(PEER_HEADS):
            e0 = e0_ref[h, ii:ii + 1, lanes]
            li = li_ref[h, ii:ii + 1, lanes]
            g = g + jnp.where(rb_ref[h, js, lanes] < li, e0 * e1_ref[h, js, lanes], 0.0)
        w_ref[rows, lanes] = (_gelu(a_ref[rows, lanes]) * g).astype(w_ref.dtype)
        return carry

    for ii in range(ib):
        lax.fori_loop(0, 2 * n_chunks, functools.partial(chunk, ii=ii), 0)

    upd = jnp.dot(vt_ref[...], w_ref[...], preferred_element_type=F32)

    @pl.when(eb == 0)
    def _():
        o_ref[...] = upd

    @pl.when(eb != 0)
    def _():
        o_ref[...] += upd


def peer_dense(h_t, u_bf16, vt_bf16, e0, li, e1, rb, *, tok0, tn, ib=8):
    d = h_t.shape[0]
    ntok = e0.shape[-1]
    ebs = ib * N_KEYS
    assert ntok % tn == 0 and tok0 % tn == 0 and N_KEYS % ib == 0
    t0 = tok0 // tn
    tok = lambda nb, eb: (0, 0, nb)
    return pl.pallas_call(
        functools.partial(_peer_dense_kernel, ib=ib),
        grid=(ntok // tn, N_KEYS // ib),
        in_specs=[pl.BlockSpec((d, tn), lambda nb, eb: (0, nb + t0)),
                  pl.BlockSpec((ebs, d), lambda nb, eb: (eb, 0)),
                  pl.BlockSpec((d, ebs), lambda nb, eb: (0, eb)),
                  pl.BlockSpec((PEER_HEADS, ib, tn), lambda nb, eb: (0, eb, nb)),
                  pl.BlockSpec((PEER_HEADS, ib, tn), lambda nb, eb: (0, eb, nb)),
                  pl.BlockSpec((PEER_HEADS, N_KEYS, tn), tok),
                  pl.BlockSpec((PEER_HEADS, N_KEYS, tn), tok)],
        out_specs=pl.BlockSpec((d, tn), lambda nb, eb: (0, nb)),
        out_shape=jax.ShapeDtypeStruct((d, ntok), F32),
        scratch_shapes=[pltpu.VMEM((ebs, tn), F32), pltpu.VMEM((ebs, tn), BF16)],
        compiler_params=_params("parallel", "arbitrary"),
        name="peer_dense",
    )(h_t, u_bf16, vt_bf16, e0, li, e1, rb)


def _add_transposed_kernel(x_ref, yt_ref, o_ref):
    o_ref[...] = x_ref[...] + jnp.transpose(yt_ref[...])


def add_transposed(x, y_t, *, tm):
    n, d = x.shape
    return pl.pallas_call(
        _add_transposed_kernel,
        grid=(n // tm,),
        in_specs=[pl.BlockSpec((tm, d), lambda i: (i, 0)), pl.BlockSpec((d, tm), lambda i: (0, i))],
        out_specs=pl.BlockSpec((tm, d), lambda i: (i, 0)),
        out_shape=jax.ShapeDtypeStruct((n, d), F32),
        compiler_params=_params("parallel"),
        name="add_transposed",
    )(x, y_t)


def peer_layer(x, norm_g, w_q, keys, u_tab, v_tab):
    h, h_t = rmsnorm(x, norm_g, BF16, tm=640, transposed=True)
    q = matmul([h], [w_q.astype(BF16)], tm=1664, tn=512)
    s_t = peer_scores(q, keys.reshape(2 * PEER_HEADS, N_KEYS, PEER_HALF).astype(BF16))
    sel_p = peer_select(s_t.reshape(2 * PEER_HEADS, N_KEYS, N_TOK // LANES, LANES),
                        rows=N_PROMPT // LANES, rblk=32)
    sel_s = peer_select(s_t[:, :, N_PROMPT:].reshape(2 * PEER_HEADS, N_KEYS, 1, DEC_BATCH),
                        rows=1, rblk=1)
    u_bf16 = u_tab.astype(BF16)
    vt_bf16 = jnp.transpose(v_tab).astype(BF16)
    out_p = peer_dense(h_t, u_bf16, vt_bf16, *sel_p, tok0=0, tn=512)
    out_s = peer_dense(h_t, u_bf16, vt_bf16, *sel_s, tok0=N_PROMPT, tn=DEC_BATCH)
    return add_transposed(x, jnp.concatenate([out_p, out_s], axis=1), tm=640)


def kernel(x_prompt, x_sample, cache_mem_k, cache_mem_v, state_pool, state_gla, mem_prompt,
           norm_mix, norm_ffn, norm_final, mem_norm, w_mem_kv,
           pool_w_in, pool_w_group, pool_scale, pool_w_out,
           gla_w_in, gla_w_gate, gla_b_gate, gla_norm, gla_w_out,
           peer_w_q, peer_keys, peer_u, peer_v):
    d = D_MODEL
    x = jnp.concatenate([x_prompt.reshape(N_PROMPT, d), x_sample.reshape(DEC_BATCH, d)], axis=0)

    mem = mem_prompt.reshape(BATCH * MEM_TOKENS, d)
    kvs = []
    for i in range(DEPTH):
        mn = rmsnorm(mem, mem_norm[i], BF16, tm=512)
        kvs.append(matmul([mn], [w_mem_kv[i].astype(BF16)], tm=512, tn=512))

    def split_kv(kv, lo):
        return kv[:, lo:lo + MEM_WIDTH].reshape(BATCH, MEM_TOKENS, MEM_HEADS, MEM_HEAD_DIM)

    mem_k_prompt = jnp.stack([split_kv(kv, 0) for kv in kvs])
    mem_v_prompt = jnp.stack([split_kv(kv, MEM_WIDTH) for kv in kvs])

    def cache(c, i):
        return c[i].reshape(DEC_BATCH, MEM_TOKENS, MEM_WIDTH)

    h = rmsnorm(x, norm_mix[0], BF16, tm=640)
    u = matmul([h], [pool_w_in[0].astype(BF16)], tm=1664, tn=512)
    us = u[N_PROMPT:]
    p_s = us[:, :POOL_WIDTH]
    mix = jnp.concatenate([
        pool_prompt(u, pool_w_group[0], pool_scale[0]),
        pool_sample(jnp.transpose(state_pool[0], (1, 0, 2)), p_s, pool_w_group[0], pool_scale[0]),
    ], axis=0)
    att = jnp.concatenate([
        attn_prompt(u, POOL_WIDTH, kvs[0]),
        attn_sample(us[:, POOL_WIDTH:].reshape(DEC_BATCH, 1, MEM_WIDTH),
                    cache(cache_mem_k, 0), cache(cache_mem_v, 0)).reshape(DEC_BATCH, MEM_WIDTH),
    ], axis=0)
    w_out = pool_w_out[0].astype(BF16)
    x = matmul([mix, att], [w_out[:POOL_WIDTH], w_out[POOL_WIDTH:]], x, tm=1664, tn=512)
    pool_prompt_state = u[:N_PROMPT, :POOL_WIDTH].reshape(BATCH, SEQ, POOL_WIDTH)[:, SEQ - POOL_BUF:][None]
    pool_sample_state = jnp.concatenate([state_pool[0][:, 1:], p_s[:, None, :]], axis=1)[None]
    x = peer_layer(x, norm_ffn[0], peer_w_q[0], peer_keys[0], peer_u[0], peer_v[0])

    w_in = gla_w_in[0]
    o3 = 2 * GLA_KEY + GLA_VAL
    o4 = o3 + GATE_RANK
    o5 = o4 + GLA_VAL
    w_in = jnp.concatenate([
        w_in[:, :o3], w_in[:, o4:o5], w_in[:, o5:], w_in[:, o3:o4],
        jnp.zeros((d, GLA_Z_PAD - GATE_RANK), w_in.dtype)], axis=1).astype(BF16)
    wg_pad = jnp.concatenate(
        [gla_w_gate[0], jnp.zeros((LANES - GATE_RANK, GLA_KEY), gla_w_gate.dtype)], axis=0).astype(BF16)
    h = rmsnorm(x, norm_mix[1], BF16, tm=640)
    u = matmul([h], [w_in], tm=1664, tn=512)
    us = u[N_PROMPT:]
    mix_p, st_t = gla_prompt(u, wg_pad, gla_b_gate[0], gla_norm[0])
    mix_s, gla_s = gla_sample(us, state_gla[0], wg_pad, gla_b_gate[0], gla_norm[0])
    mix = jnp.concatenate([mix_p, mix_s.astype(BF16)], axis=0)
    att = jnp.concatenate([
        attn_prompt(u, GLA_COL_QM, kvs[1]),
        attn_sample(us[:, GLA_COL_QM:GLA_COL_QM + MEM_WIDTH].reshape(DEC_BATCH, 1, MEM_WIDTH),
                    cache(cache_mem_k, 1), cache(cache_mem_v, 1)).reshape(DEC_BATCH, MEM_WIDTH),
    ], axis=0)
    w_out = gla_w_out[0].astype(BF16)
    x = matmul([mix, att], [w_out[:GLA_VAL], w_out[GLA_VAL:]], x, tm=1664, tn=512)
    x = peer_layer(x, norm_ffn[1], peer_w_q[1], peer_keys[1], peer_u[1], peer_v[1])

    y_prompt = rmsnorm(x, norm_final, F32, tm=512, row0=0, rows=N_PROMPT)
    y_sample = rmsnorm(x, norm_final, F32, tm=DEC_BATCH, row0=N_PROMPT, rows=DEC_BATCH)
    return (y_prompt.reshape(BATCH, SEQ, d),
            y_sample.reshape(DEC_BATCH, 1, d),
            pool_prompt_state,
            jnp.swapaxes(st_t, -1, -2)[None],
            mem_k_prompt,
            mem_v_prompt,
            pool_sample_state,
            gla_s[None])
```

```python
import functools

import jax
import jax.numpy as jnp
from jax import lax
from jax.experimental import pallas as pl
from jax.experimental.pallas import tpu as pltpu

D_MODEL = 2048
BATCH = 4
SEQ = 2048
DEPTH = 2
DEC_BATCH = 128
PAST_LEN = 16384
EPS = 1e-6

N_PROMPT = BATCH * SEQ
N_TOK = N_PROMPT + DEC_BATCH

POOL_WIDTH = D_MODEL // 2
POOL_GROUPS = 4
POOL_GC = POOL_WIDTH // POOL_GROUPS
POOL_WINDOWS = (2, 4, 8, 16)
POOL_BUF = max(POOL_WINDOWS) - 1

MEM_TOKENS = 256
MEM_HEADS = 4
MEM_HEAD_DIM = D_MODEL // 8
MEM_WIDTH = MEM_HEADS * MEM_HEAD_DIM
MEM_SCALE = MEM_HEAD_DIM ** -0.5

GLA_HEADS = 4
GLA_KEY = D_MODEL // 2
GLA_VAL = D_MODEL
GLA_DK = GLA_KEY // GLA_HEADS
GLA_DV = GLA_VAL // GLA_HEADS
GATE_RANK = 16
GATE_TAU = 16.0
GLA_CHUNK = 64
GLA_SCALE = GLA_DK ** -0.5

N_KEYS = 128
N_EXPERTS = N_KEYS * N_KEYS
PEER_HEADS = 8
PEER_HALF = 128
PEER_TOPK = 16

LANES = 128
SUBLANES = 8
VMEM_LIMIT = 56 * 1024 * 1024

GLA_COL_Q = 0
GLA_COL_K = GLA_KEY
GLA_COL_V = 2 * GLA_KEY
GLA_COL_OG = 2 * GLA_KEY + GLA_VAL
GLA_COL_QM = 2 * GLA_KEY + 2 * GLA_VAL
GLA_COL_Z = GLA_COL_QM + MEM_WIDTH
GLA_Z_PAD = 512
GLA_IN_COLS = GLA_COL_Z + GLA_Z_PAD

BF16 = jnp.bfloat16
F32 = jnp.float32

_NT = (((1,), (1,)), ((), ()))
_TN = (((0,), (0,)), ((), ()))


def _params(*sem):
    return pltpu.CompilerParams(dimension_semantics=sem, vmem_limit_bytes=VMEM_LIMIT)


def _rmsnorm_kernel(x_ref, g_ref, *o_refs):
    x = x_ref[...]
    y = x * lax.rsqrt(jnp.mean(x * x, axis=-1, keepdims=True) + EPS) * g_ref[...]
    o_refs[0][...] = y.astype(o_refs[0].dtype)
    if len(o_refs) > 1:
        o_refs[1][...] = jnp.transpose(y).astype(o_refs[1].dtype)


def rmsnorm(x, g, out_dtype, *, tm, row0=0, rows=None, transposed=False):
    n, d = x.shape
    rows = n - row0 if rows is None else rows
    assert rows % tm == 0 and row0 % tm == 0
    blk0 = row0 // tm
    out_shape = [jax.ShapeDtypeStruct((rows, d), out_dtype)]
    out_specs = [pl.BlockSpec((tm, d), lambda i: (i, 0))]
    if transposed:
        out_shape.append(jax.ShapeDtypeStruct((d, rows), out_dtype))
        out_specs.append(pl.BlockSpec((d, tm), lambda i: (0, i)))
    res = pl.pallas_call(
        _rmsnorm_kernel,
        grid=(rows // tm,),
        in_specs=[pl.BlockSpec((tm, d), lambda i: (i + blk0, 0)),
                  pl.BlockSpec((1, d), lambda i: (0, 0))],
        out_specs=out_specs,
        out_shape=out_shape,
        compiler_params=_params("parallel"),
        name="rmsnorm",
    )(x, g.reshape(1, d))
    return res if transposed else res[0]


def _matmul_kernel(*refs, n_pairs, has_res):
    o_ref = refs[-1]
    acc = None
    for a_ref, w_ref in zip(refs[:n_pairs], refs[n_pairs:2 * n_pairs]):
        p = jnp.dot(a_ref[...], w_ref[...], preferred_element_type=F32)
        acc = p if acc is None else acc + p
    if has_res:
        acc = acc + refs[2 * n_pairs][...]
    o_ref[...] = acc.astype(o_ref.dtype)


def matmul(a_list, w_list, residual=None, *, tm, tn):
    n = a_list[0].shape[0]
    f = w_list[0].shape[1]
    assert n % tm == 0 and f % tn == 0
    in_specs = [pl.BlockSpec((tm, a.shape[1]), lambda i, j: (i, 0)) for a in a_list]
    in_specs += [pl.BlockSpec((w.shape[0], tn), lambda i, j: (0, j)) for w in w_list]
    args = list(a_list) + list(w_list)
    if residual is not None:
        in_specs.append(pl.BlockSpec((tm, tn), lambda i, j: (i, j)))
        args.append(residual)
    return pl.pallas_call(
        functools.partial(_matmul_kernel, n_pairs=len(a_list), has_res=residual is not None),
        grid=(n // tm, f // tn),
        in_specs=in_specs,
        out_specs=pl.BlockSpec((tm, tn), lambda i, j: (i, j)),
        out_shape=jax.ShapeDtypeStruct((n, f), F32),
        compiler_params=_params("parallel", "parallel"),
        name="matmul",
    )(*args)


def _pool_prompt_kernel(p_ref, wg_ref, sc_ref, o_ref):
    g = pl.program_id(1)
    x = p_ref[...]
    t_len = x.shape[0]
    row = lax.broadcasted_iota(jnp.int32, (t_len, 1), 0)

    def shifted(v, k):
        return jnp.where(row >= k, pltpu.roll(v, k, 0), 0.0)

    for gi, w in enumerate(POOL_WINDOWS):
        @pl.when(g == gi)
        def _(w=w):
            s = x
            k = 1
            while k < w:
                s = s + shifted(s, k)
                k *= 2
            cnt = jnp.minimum(w, row + 1).astype(F32)
            d = s / cnt - x
            out = jnp.dot(d.astype(BF16), wg_ref[0], preferred_element_type=F32)
            o_ref[...] = (out * sc_ref[0]).astype(o_ref.dtype)


def pool_prompt(u, w_group, scale):
    return pl.pallas_call(
        _pool_prompt_kernel,
        grid=(BATCH, POOL_GROUPS),
        in_specs=[pl.BlockSpec((SEQ, POOL_GC), lambda b, g: (b, g)),
                  pl.BlockSpec((1, POOL_GC, POOL_GC), lambda b, g: (g, 0, 0)),
                  pl.BlockSpec((1, 1, POOL_GC), lambda b, g: (g, 0, 0))],
        out_specs=pl.BlockSpec((SEQ, POOL_GC), lambda b, g: (b, g)),
        out_shape=jax.ShapeDtypeStruct((N_PROMPT, POOL_WIDTH), BF16),
        compiler_params=_params("parallel", "parallel"),
        name="pool_prompt",
    )(u, w_group.astype(BF16), scale.reshape(POOL_GROUPS, 1, POOL_GC))


def _pool_sample_kernel(prev_ref, p_ref, wg_ref, sc_ref, o_ref):
    g = pl.program_id(0)
    x = p_ref[...]
    for gi, w in enumerate(POOL_WINDOWS):
        @pl.when(g == gi)
        def _(w=w):
            s = x
            for r in range(POOL_BUF - (w - 1), POOL_BUF):
                s = s + prev_ref[r]
            cnt = float(min(w, PAST_LEN + 1))
            d = s / cnt - x
            out = jnp.dot(d.astype(BF16), wg_ref[0], preferred_element_type=F32)
            o_ref[...] = (out * sc_ref[0]).astype(o_ref.dtype)


def pool_sample(prev_t, p, w_group, scale):
    b = p.shape[0]
    return pl.pallas_call(
        _pool_sample_kernel,
        grid=(POOL_GROUPS,),
        in_specs=[pl.BlockSpec((POOL_BUF, b, POOL_GC), lambda g: (0, 0, g)),
                  pl.BlockSpec((b, POOL_GC), lambda g: (0, g)),
                  pl.BlockSpec((1, POOL_GC, POOL_GC), lambda g: (g, 0, 0)),
                  pl.BlockSpec((1, 1, POOL_GC), lambda g: (g, 0, 0))],
        out_specs=pl.BlockSpec((b, POOL_GC), lambda g: (0, g)),
        out_shape=jax.ShapeDtypeStruct((b, POOL_WIDTH), BF16),
        compiler_params=_params("parallel"),
        name="pool_sample",
    )(prev_t, p, w_group.astype(BF16), scale.reshape(POOL_GROUPS, 1, POOL_GC))


def _attn_prompt_kernel(q_ref, kv_ref, o_ref):
    for h in range(MEM_HEADS):
        lo, hi = h * MEM_HEAD_DIM, (h + 1) * MEM_HEAD_DIM
        qh = q_ref[:, lo:hi].astype(BF16)
        kh = kv_ref[:, lo:hi].astype(BF16)
        vh = kv_ref[:, MEM_WIDTH + lo:MEM_WIDTH + hi].astype(BF16)
        s = lax.dot_general(qh, kh, _NT, preferred_element_type=F32) * MEM_SCALE
        e = jnp.exp(s - jnp.max(s, axis=-1, keepdims=True))
        l = jnp.sum(e, axis=-1, keepdims=True)
        o = jnp.dot(e.astype(BF16), vh, preferred_element_type=F32) / l
        o_ref[:, lo:hi] = o.astype(o_ref.dtype)


def attn_prompt(u, q_col, kv, *, tq=512):
    assert q_col % MEM_WIDTH == 0
    nt = SEQ // tq
    return pl.pallas_call(
        _attn_prompt_kernel,
        grid=(BATCH, nt),
        in_specs=[pl.BlockSpec((tq, MEM_WIDTH), lambda b, t: (b * nt + t, q_col // MEM_WIDTH)),
                  pl.BlockSpec((MEM_TOKENS, 2 * MEM_WIDTH), lambda b, t: (b, 0))],
        out_specs=pl.BlockSpec((tq, MEM_WIDTH), lambda b, t: (b * nt + t, 0)),
        out_shape=jax.ShapeDtypeStruct((N_PROMPT, MEM_WIDTH), BF16),
        compiler_params=_params("parallel", "parallel"),
        name="attn_prompt",
    )(u, kv)


def _attn_sample_kernel(q_ref, k_ref, v_ref, o_ref):
    for i in range(q_ref.shape[0]):
        prod = k_ref[i] * q_ref[i]
        vb = v_ref[i]
        for h in range(MEM_HEADS):
            lo, hi = h * MEM_HEAD_DIM, (h + 1) * MEM_HEAD_DIM
            s = jnp.sum(prod[:, lo:hi], axis=-1, keepdims=True) * MEM_SCALE
            e = jnp.exp(s - jnp.max(s, axis=0, keepdims=True))
            p = e / jnp.sum(e, axis=0, keepdims=True)
            o = jnp.sum(p * vb[:, lo:hi], axis=0, keepdims=True)
            o_ref[i, :, lo:hi] = o.astype(o_ref.dtype)


def attn_sample(q, mem_k, mem_v, *, bb=4):
    b = q.shape[0]
    return pl.pallas_call(
        _attn_sample_kernel,
        grid=(b // bb,),
        in_specs=[pl.BlockSpec((bb, 1, MEM_WIDTH), lambda i: (i, 0, 0)),
                  pl.BlockSpec((bb, MEM_TOKENS, MEM_WIDTH), lambda i: (i, 0, 0)),
                  pl.BlockSpec((bb, MEM_TOKENS, MEM_WIDTH), lambda i: (i, 0, 0))],
        out_specs=pl.BlockSpec((bb, 1, MEM_WIDTH), lambda i: (i, 0, 0)),
        out_shape=jax.ShapeDtypeStruct((b, 1, MEM_WIDTH), BF16),
        compiler_params=_params("parallel"),
        name="attn_sample",
    )(q, mem_k, mem_v)


def _log_decay(z, wg_ref, bg_ref):
    g = jnp.dot(z.astype(BF16), wg_ref[...], preferred_element_type=F32) + bg_ref[...]
    return jax.nn.log_sigmoid(g) / GATE_TAU


def _gla_out(o, og, gn):
    on = o * lax.rsqrt(jnp.mean(o * o, axis=-1, keepdims=True) + EPS) * gn
    return on * (og * jax.nn.sigmoid(og))


def _gla_prompt_kernel(q_ref, k_ref, v_ref, og_ref, z_ref, wg_ref, bg_ref, gn_ref,
                       mix_ref, st_ref, s_ref):
    n = pl.program_id(1)
    c = q_ref.shape[0]

    @pl.when(n == 0)
    def _():
        s_ref[...] = jnp.zeros_like(s_ref)

    la = _log_decay(z_ref[...], wg_ref, bg_ref)
    row = lax.broadcasted_iota(jnp.int32, (c, 1), 0)
    b = la
    k = 1
    while k < c:
        b = b + jnp.where(row >= k, pltpu.roll(b, k, 0), 0.0)
        k *= 2
    b_last = b[c - 1:c, :]
    q_in = q_ref[...] * jnp.exp(b) * GLA_SCALE
    k_in = k_ref[...] * jnp.exp(-b)
    k_out = k_ref[...] * jnp.exp(b_last - b)
    decay = jnp.exp(b_last)
    causal = row >= lax.broadcasted_iota(jnp.int32, (1, c), 1)
    gn = gn_ref[...]
    for h in range(GLA_HEADS):
        klo, khi = h * GLA_DK, (h + 1) * GLA_DK
        vlo, vhi = h * GLA_DV, (h + 1) * GLA_DV
        qh = q_in[:, klo:khi].astype(BF16)
        vh = v_ref[:, vlo:vhi].astype(BF16)
        a = lax.dot_general(qh, k_in[:, klo:khi].astype(BF16), _NT, preferred_element_type=F32)
        a = jnp.where(causal, a, 0.0)
        st = s_ref[h]
        o = (jnp.dot(a.astype(BF16), vh, preferred_element_type=F32)
             + lax.dot_general(qh, st.astype(BF16), _NT, preferred_element_type=F32))
        s_ref[h] = st * decay[:, klo:khi] + lax.dot_general(
            vh, k_out[:, klo:khi].astype(BF16), _TN, preferred_element_type=F32)
        mix_ref[:, vlo:vhi] = _gla_out(o, og_ref[:, vlo:vhi], gn).astype(mix_ref.dtype)

    @pl.when(n == pl.num_programs(1) - 1)
    def _():
        st_ref[0] = s_ref[...]


def gla_prompt(u, wg_pad, b_gate, norm_g):
    nc = SEQ // GLA_CHUNK
    c = GLA_CHUNK

    def col(width, off):
        assert off % width == 0
        return pl.BlockSpec((c, width), lambda b, n: (b * nc + n, off // width))

    return pl.pallas_call(
        _gla_prompt_kernel,
        grid=(BATCH, nc),
        in_specs=[col(GLA_KEY, GLA_COL_Q), col(GLA_KEY, GLA_COL_K), col(GLA_VAL, GLA_COL_V),
                  col(GLA_VAL, GLA_COL_OG), col(LANES, GLA_COL_Z),
                  pl.BlockSpec((LANES, GLA_KEY), lambda b, n: (0, 0)),
                  pl.BlockSpec((1, GLA_KEY), lambda b, n: (0, 0)),
                  pl.BlockSpec((1, GLA_DV), lambda b, n: (0, 0))],
        out_specs=[pl.BlockSpec((c, GLA_VAL), lambda b, n: (b * nc + n, 0)),
                   pl.BlockSpec((1, GLA_HEADS, GLA_DV, GLA_DK), lambda b, n: (b, 0, 0, 0))],
        out_shape=[jax.ShapeDtypeStruct((N_PROMPT, GLA_VAL), BF16),
                   jax.ShapeDtypeStruct((BATCH, GLA_HEADS, GLA_DV, GLA_DK), F32)],
        scratch_shapes=[pltpu.VMEM((GLA_HEADS, GLA_DV, GLA_DK), F32)],
        compiler_params=_params("parallel", "arbitrary"),
        name="gla_prompt",
    )(u, u, u, u, u, wg_pad, b_gate.reshape(1, GLA_KEY), norm_g.reshape(1, GLA_DV))


def _gla_sample_kernel(us_ref, s0_ref, wg_ref, bg_ref, gn_ref, mix_ref, s1_ref, la_ref):
    i = pl.program_id(0)

    @pl.when(i == 0)
    def _():
        la_ref[...] = _log_decay(us_ref[:, GLA_COL_Z:GLA_COL_Z + LANES], wg_ref, bg_ref)

    tok = pl.ds(i, 1)
    gn = gn_ref[...]
    for h in range(GLA_HEADS):
        klo, khi = h * GLA_DK, (h + 1) * GLA_DK
        vlo, vhi = h * GLA_DV, (h + 1) * GLA_DV
        la = la_ref[tok, klo:khi]
        q = us_ref[tok, GLA_COL_Q + klo:GLA_COL_Q + khi]
        k = us_ref[tok, GLA_COL_K + klo:GLA_COL_K + khi]
        v = us_ref[tok, GLA_COL_V + vlo:GLA_COL_V + vhi]
        og = us_ref[tok, GLA_COL_OG + vlo:GLA_COL_OG + vhi]
        q_in = q * jnp.exp(la) * GLA_SCALE
        k_in = k * jnp.exp(-la)
        k_out = k * jnp.exp(la - la)
        decay = jnp.exp(la)
        a = jnp.sum(q_in * k_in, axis=-1, keepdims=True)
        rows = jnp.concatenate(
            [q_in, k_out, decay, jnp.zeros((SUBLANES - 3, GLA_DK), F32)], axis=0)
        cols = jnp.transpose(rows)
        s0 = s0_ref[0, h]
        o = a * v + jnp.sum(cols[:, 0:1] * s0, axis=0, keepdims=True)
        s1_ref[0, h] = s0 * cols[:, 2:3] + cols[:, 1:2] * v
        mix_ref[tok, vlo:vhi] = _gla_out(o, og, gn).astype(mix_ref.dtype)


def gla_sample(us, s0, wg_pad, b_gate, norm_g):
    b = us.shape[0]
    return pl.pallas_call(
        _gla_sample_kernel,
        grid=(b,),
        in_specs=[pl.BlockSpec((b, GLA_IN_COLS), lambda i: (0, 0)),
                  pl.BlockSpec((1, GLA_HEADS, GLA_DK, GLA_DV), lambda i: (i, 0, 0, 0)),
                  pl.BlockSpec((LANES, GLA_KEY), lambda i: (0, 0)),
                  pl.BlockSpec((1, GLA_KEY), lambda i: (0, 0)),
                  pl.BlockSpec((1, GLA_DV), lambda i: (0, 0))],
        out_specs=[pl.BlockSpec((b, GLA_VAL), lambda i: (0, 0)),
                   pl.BlockSpec((1, GLA_HEADS, GLA_DK, GLA_DV), lambda i: (i, 0, 0, 0))],
        out_shape=[jax.ShapeDtypeStruct((b, GLA_VAL), F32),
                   jax.ShapeDtypeStruct(s0.shape, F32)],
        scratch_shapes=[pltpu.VMEM((b, GLA_KEY), F32)],
        compiler_params=_params("arbitrary"),
        name="gla_sample",
    )(us, s0, wg_pad, b_gate.reshape(1, GLA_KEY), norm_g.reshape(1, GLA_DV))


def _peer_score_kernel(q_ref, keys_ref, o_ref):
    for hp in range(2 * PEER_HEADS):
        qs = q_ref[:, hp * PEER_HALF:(hp + 1) * PEER_HALF].astype(BF16)
        o_ref[hp] = lax.dot_general(keys_ref[hp], qs, _NT, preferred_element_type=F32)


def peer_scores(q, keys_bf16, *, tn=640):
    n = q.shape[0]
    assert n % tn == 0
    return pl.pallas_call(
        _peer_score_kernel,
        grid=(n // tn,),
        in_specs=[pl.BlockSpec((tn, q.shape[1]), lambda i: (i, 0)),
                  pl.BlockSpec((2 * PEER_HEADS, N_KEYS, PEER_HALF), lambda i: (0, 0, 0))],
        out_specs=pl.BlockSpec((2 * PEER_HEADS, N_KEYS, tn), lambda i: (0, 0, i)),
        out_shape=jax.ShapeDtypeStruct((2 * PEER_HEADS, N_KEYS, n), F32),
        compiler_params=_params("parallel"),
        name="peer_scores",
    )(q, keys_bf16)


_PEER_CAND = tuple((a, b) for a in range(PEER_TOPK) for b in range(PEER_TOPK)
                   if (a + 1) * (b + 1) <= PEER_TOPK)
_NEG_INF = float("-inf")
_UNRANKED = float(N_KEYS)


_REDUCE_WIDTH = 8


def _reduce(op, vals):
    accs = list(vals[:_REDUCE_WIDTH])
    for n, v in enumerate(vals[_REDUCE_WIDTH:]):
        accs[n % _REDUCE_WIDTH] = op(accs[n % _REDUCE_WIDTH], v)
    while len(accs) > 1:
        accs = [op(accs[n], accs[n + 1]) if n + 1 < len(accs) else accs[n]
                for n in range(0, len(accs), 2)]
    return accs[0]


def _peer_select_kernel(s_ref, e0_ref, li_ref, e1_ref, rb_ref,
                        cur_ref, sv_ref, ix_ref, cand_ref, cnt_ref):
    tile = s_ref.shape[2:]

    for p in range(2):
        cur_ref[...] = s_ref[p]
        if p == 1:
            rb_ref[0] = jnp.full((N_KEYS,) + tile, _UNRANKED, F32)

        def extract(r, m, p=p):
            idx = _reduce(jnp.minimum, [jnp.where(cur_ref[k] == m, float(k), _UNRANKED)
                                        for k in range(N_KEYS)])
            sv_ref[p, pl.ds(r, 1)] = m[None]
            if p == 0:
                ix_ref[pl.ds(r, 1)] = idx[None]
            rf = r.astype(F32)
            rest = []
            for k in range(N_KEYS):
                hit = idx == float(k)
                c = jnp.where(hit, _NEG_INF, cur_ref[k])
                cur_ref[k] = c
                if p == 1:
                    rb_ref[0, k] = jnp.where(hit, rf, rb_ref[0, k])
                rest.append(c)
            return _reduce(jnp.maximum, rest)

        lax.fori_loop(0, PEER_TOPK, extract,
                      _reduce(jnp.maximum, [s_ref[p, k] for k in range(N_KEYS)]))

    for ci, (a, b) in enumerate(_PEER_CAND):
        cand_ref[ci] = sv_ref[0, a] + sv_ref[1, b]
    cnt_ref[...] = jnp.zeros_like(cnt_ref)
    c_max = sv_ref[0, 0] + sv_ref[1, 0]
    no_flat = float(PEER_TOPK * PEER_TOPK)

    def pick(_, z):
        m = _reduce(jnp.maximum, [cand_ref[ci] for ci in range(len(_PEER_CAND))])
        flat = _reduce(jnp.minimum, [
            jnp.where(cand_ref[ci] == m, float(a * PEER_TOPK + b), no_flat)
            for ci, (a, b) in enumerate(_PEER_CAND)])
        for ci, (a, b) in enumerate(_PEER_CAND):
            cand_ref[ci] = jnp.where(flat == float(a * PEER_TOPK + b), _NEG_INF, cand_ref[ci])
        first = jnp.floor(flat * (1.0 / PEER_TOPK))
        for a in range(PEER_TOPK):
            cnt_ref[a] = cnt_ref[a] + jnp.where(first == float(a), 1.0, 0.0)
        return z + jnp.exp(m - c_max)

    inv_z = 1.0 / lax.fori_loop(0, PEER_TOPK, pick, jnp.zeros(tile, F32))

    max0 = sv_ref[0, 0]
    max1 = sv_ref[1, 0]

    def emit(k, carry):
        kf = k.astype(F32)
        li = jnp.zeros(tile, F32)
        for a in range(PEER_TOPK):
            li = jnp.where(ix_ref[a] == kf, cnt_ref[a], li)
        li_ref[0, k] = li
        e0_ref[0, k] = jnp.exp(s_ref[0, k] - max0) * inv_z
        e1_ref[0, k] = jnp.exp(s_ref[1, k] - max1)
        return carry

    lax.fori_loop(0, N_KEYS, emit, 0, unroll=4)


def peer_select(s4, *, rows, rblk):
    assert rows % rblk == 0 and (rblk % SUBLANES == 0 or rblk == s4.shape[2])
    tile = (rblk, LANES)
    out = jax.ShapeDtypeStruct((PEER_HEADS, N_KEYS, rows, LANES), F32)
    out_spec = pl.BlockSpec((1, N_KEYS) + tile, lambda h, r: (h, 0, r, 0))
    res = pl.pallas_call(
        _peer_select_kernel,
        grid=(PEER_HEADS, rows // rblk),
        in_specs=[pl.BlockSpec((2, N_KEYS) + tile, lambda h, r: (h, 0, r, 0))],
        out_specs=[out_spec] * 4,
        out_shape=[out] * 4,
        scratch_shapes=[pltpu.VMEM((N_KEYS,) + tile, F32),
                        pltpu.VMEM((2, PEER_TOPK) + tile, F32),
                        pltpu.VMEM((PEER_TOPK,) + tile, F32),
                        pltpu.VMEM((len(_PEER_CAND),) + tile, F32),
                        pltpu.VMEM((PEER_TOPK,) + tile, F32)],
        compiler_params=_params("parallel", "parallel"),
        name="peer_select",
    )(s4)
    return [r.reshape(PEER_HEADS, N_KEYS, rows * LANES) for r in res]


_DENSE_CHUNK_KEYS = 64


def _gelu(x):
    return 0.5 * x * (1.0 + lax.erf(x * (2.0 ** -0.5)))


def _peer_dense_kernel(ht_ref, u_ref, vt_ref, e0_ref, li_ref, e1_ref, rb_ref, o_ref,
                       a_ref, w_ref, *, ib):
    eb = pl.program_id(1)
    tn = ht_ref.shape[1]

    a_ref[...] = jnp.dot(u_ref[...], ht_ref[...], preferred_element_type=F32)

    jc = _DENSE_CHUNK_KEYS
    for ii in range(ib):
        for j0 in range(0, N_KEYS, jc):
            for c0 in range(0, tn, LANES):
                lanes = slice(c0, c0 + LANES)
                js = slice(j0, j0 + jc)
                rows = slice(ii * N_KEYS + j0, ii * N_KEYS + j0 + jc)
                g = None
                for h in range(PEER_HEADS):
                    e0 = e0_ref[h, ii:ii + 1, lanes]
                    li = li_ref[h, ii:ii + 1, lanes]
                    t = jnp.where(rb_ref[h, js, lanes] < li, e0 * e1_ref[h, js, lanes], 0.0)
                    g = t if g is None else g + t
                w_ref[rows, lanes] = (_gelu(a_ref[rows, lanes]) * g).astype(w_ref.dtype)

    upd = jnp.dot(vt_ref[...], w_ref[...], preferred_element_type=F32)

    @pl.when(eb == 0)
    def _():
        o_ref[...] = upd

    @pl.when(eb != 0)
    def _():
        o_ref[...] += upd


def peer_dense(h_t, u_bf16, vt_bf16, e0, li, e1, rb, *, tok0, tn, ib=8):
    d = h_t.shape[0]
    ntok = e0.shape[-1]
    ebs = ib * N_KEYS
    assert ntok % tn == 0 and tok0 % tn == 0 and N_KEYS % ib == 0
    t0 = tok0 // tn
    tok = lambda nb, eb: (0, 0, nb)
    return pl.pallas_call(
        functools.partial(_peer_dense_kernel, ib=ib),
        grid=(ntok // tn, N_KEYS // ib),
        in_specs=[pl.BlockSpec((d, tn), lambda nb, eb: (0, nb + t0)),
                  pl.BlockSpec((ebs, d), lambda nb, eb: (eb, 0)),
                  pl.BlockSpec((d, ebs), lambda nb, eb: (0, eb)),
                  pl.BlockSpec((PEER_HEADS, ib, tn), lambda nb, eb: (0, eb, nb)),
                  pl.BlockSpec((PEER_HEADS, ib, tn), lambda nb, eb: (0, eb, nb)),
                  pl.BlockSpec((PEER_HEADS, N_KEYS, tn), tok),
                  pl.BlockSpec((PEER_HEADS, N_KEYS, tn), tok)],
        out_specs=pl.BlockSpec((d, tn), lambda nb, eb: (0, nb)),
        out_shape=jax.ShapeDtypeStruct((d, ntok), F32),
        scratch_shapes=[pltpu.VMEM((ebs, tn), F32), pltpu.VMEM((ebs, tn), BF16)],
        compiler_params=_params("parallel", "arbitrary"),
        name="peer_dense",
    )(h_t, u_bf16, vt_bf16, e0, li, e1, rb)


def _add_transposed_kernel(x_ref, yt_ref, o_ref):
    o_ref[...] = x_ref[...] + jnp.transpose(yt_ref[...])


def add_transposed(x, y_t, *, tm):
    n, d = x.shape
    return pl.pallas_call(
        _add_transposed_kernel,
        grid=(n // tm,),
        in_specs=[pl.BlockSpec((tm, d), lambda i: (i, 0)), pl.BlockSpec((d, tm), lambda i: (0, i))],
        out_specs=pl.BlockSpec((tm, d), lambda i: (i, 0)),
        out_shape=jax.ShapeDtypeStruct((n, d), F32),
        compiler_params=_params("parallel"),
        name="add_transposed",
    )(x, y_t)


def peer_layer(x, norm_g, w_q, keys, u_tab, v_tab):
    h, h_t = rmsnorm(x, norm_g, BF16, tm=640, transposed=True)
    q = matmul([h], [w_q.astype(BF16)], tm=1664, tn=512)
    s_t = peer_scores(q, keys.reshape(2 * PEER_HEADS, N_KEYS, PEER_HALF).astype(BF16))
    sel_p = peer_select(s_t.reshape(2 * PEER_HEADS, N_KEYS, N_TOK // LANES, LANES),
                        rows=N_PROMPT // LANES, rblk=32)
    sel_s = peer_select(s_t[:, :, N_PROMPT:].reshape(2 * PEER_HEADS, N_KEYS, 1, DEC_BATCH),
                        rows=1, rblk=1)
    u_bf16 = u_tab.astype(BF16)
    vt_bf16 = jnp.transpose(v_tab).astype(BF16)
    out_p = peer_dense(h_t, u_bf16, vt_bf16, *sel_p, tok0=0, tn=512)
    out_s = peer_dense(h_t, u_bf16, vt_bf16, *sel_s, tok0=N_PROMPT, tn=DEC_BATCH)
    return add_transposed(x, jnp.concatenate([out_p, out_s], axis=1), tm=640)


def kernel(x_prompt, x_sample, cache_mem_k, cache_mem_v, state_pool, state_gla, mem_prompt,
           norm_mix, norm_ffn, norm_final, mem_norm, w_mem_kv,
           pool_w_in, pool_w_group, pool_scale, pool_w_out,
           gla_w_in, gla_w_gate, gla_b_gate, gla_norm, gla_w_out,
           peer_w_q, peer_keys, peer_u, peer_v):
    d = D_MODEL
    x = jnp.concatenate([x_prompt.reshape(N_PROMPT, d), x_sample.reshape(DEC_BATCH, d)], axis=0)

    mem = mem_prompt.reshape(BATCH * MEM_TOKENS, d)
    kvs = []
    for i in range(DEPTH):
        mn = rmsnorm(mem, mem_norm[i], BF16, tm=512)
        kvs.append(matmul([mn], [w_mem_kv[i].astype(BF16)], tm=512, tn=512))

    def split_kv(kv, lo):
        return kv[:, lo:lo + MEM_WIDTH].reshape(BATCH, MEM_TOKENS, MEM_HEADS, MEM_HEAD_DIM)

    mem_k_prompt = jnp.stack([split_kv(kv, 0) for kv in kvs])
    mem_v_prompt = jnp.stack([split_kv(kv, MEM_WIDTH) for kv in kvs])

    def cache(c, i):
        return c[i].reshape(DEC_BATCH, MEM_TOKENS, MEM_WIDTH)

    h = rmsnorm(x, norm_mix[0], BF16, tm=640)
    u = matmul([h], [pool_w_in[0].astype(BF16)], tm=1664, tn=512)
    us = u[N_PROMPT:]
    p_s = us[:, :POOL_WIDTH]
    mix = jnp.concatenate([
        pool_prompt(u, pool_w_group[0], pool_scale[0]),
        pool_sample(jnp.transpose(state_pool[0], (1, 0, 2)), p_s, pool_w_group[0], pool_scale[0]),
    ], axis=0)
    att = jnp.concatenate([
        attn_prompt(u, POOL_WIDTH, kvs[0]),
        attn_sample(us[:, POOL_WIDTH:].reshape(DEC_BATCH, 1, MEM_WIDTH),
                    cache(cache_mem_k, 0), cache(cache_mem_v, 0)).reshape(DEC_BATCH, MEM_WIDTH),
    ], axis=0)
    w_out = pool_w_out[0].astype(BF16)
    x = matmul([mix, att], [w_out[:POOL_WIDTH], w_out[POOL_WIDTH:]], x, tm=1664, tn=512)
    pool_prompt_state = u[:N_PROMPT, :POOL_WIDTH].reshape(BATCH, SEQ, POOL_WIDTH)[:, SEQ - POOL_BUF:][None]
    pool_sample_state = jnp.concatenate([state_pool[0][:, 1:], p_s[:, None, :]], axis=1)[None]
    x = peer_layer(x, norm_ffn[0], peer_w_q[0], peer_keys[0], peer_u[0], peer_v[0])

    w_in = gla_w_in[0]
    o3 = 2 * GLA_KEY + GLA_VAL
    o4 = o3 + GATE_RANK
    o5 = o4 + GLA_VAL
    w_in = jnp.concatenate([
        w_in[:, :o3], w_in[:, o4:o5], w_in[:, o5:], w_in[:, o3:o4],
        jnp.zeros((d, GLA_Z_PAD - GATE_RANK), w_in.dtype)], axis=1).astype(BF16)
    wg_pad = jnp.concatenate(
        [gla_w_gate[0], jnp.zeros((LANES - GATE_RANK, GLA_KEY), gla_w_gate.dtype)], axis=0).astype(BF16)
    h = rmsnorm(x, norm_mix[1], BF16, tm=640)
    u = matmul([h], [w_in], tm=1664, tn=512)
    us = u[N_PROMPT:]
    mix_p, st_t = gla_prompt(u, wg_pad, gla_b_gate[0], gla_norm[0])
    mix_s, gla_s = gla_sample(us, state_gla[0], wg_pad, gla_b_gate[0], gla_norm[0])
    mix = jnp.concatenate([mix_p, mix_s.astype(BF16)], axis=0)
    att = jnp.concatenate([
        attn_prompt(u, GLA_COL_QM, kvs[1]),
        attn_sample(us[:, GLA_COL_QM:GLA_COL_QM + MEM_WIDTH].reshape(DEC_BATCH, 1, MEM_WIDTH),
                    cache(cache_mem_k, 1), cache(cache_mem_v, 1)).reshape(DEC_BATCH, MEM_WIDTH),
    ], axis=0)
    w_out = gla_w_out[0].astype(BF16)
    x = matmul([mix, att], [w_out[:GLA_VAL], w_out[GLA_VAL:]], x, tm=1664, tn=512)
    x = peer_layer(x, norm_ffn[1], peer_w_q[1], peer_keys[1], peer_u[1], peer_v[1])

    y_prompt = rmsnorm(x, norm_final, F32, tm=512, row0=0, rows=N_PROMPT)
    y_sample = rmsnorm(x, norm_final, F32, tm=DEC_BATCH, row0=N_PROMPT, rows=DEC_BATCH)
    return (y_prompt.reshape(BATCH, SEQ, d),
            y_sample.reshape(DEC_BATCH, 1, d),
            pool_prompt_state,
            jnp.swapaxes(st_t, -1, -2)[None],
            mem_k_prompt,
            mem_v_prompt,
            pool_sample_state,
            gla_s[None])
```

```python
import functools

import jax
import jax.numpy as jnp
from jax import lax
from jax.experimental import pallas as pl
from jax.experimental.pallas import tpu as pltpu

D_MODEL = 2048
BATCH = 4
SEQ = 2048
DEPTH = 2
DEC_BATCH = 128
PAST_LEN = 16384
EPS = 1e-6

N_PROMPT = BATCH * SEQ
N_TOK = N_PROMPT + DEC_BATCH

POOL_WIDTH = D_MODEL // 2
POOL_GROUPS = 4
POOL_GC = POOL_WIDTH // POOL_GROUPS
POOL_WINDOWS = (2, 4, 8, 16)
POOL_BUF = max(POOL_WINDOWS) - 1

MEM_TOKENS = 256
MEM_HEADS = 4
MEM_HEAD_DIM = D_MODEL // 8
MEM_WIDTH = MEM_HEADS * MEM_HEAD_DIM
MEM_SCALE = MEM_HEAD_DIM ** -0.5

GLA_HEADS = 4
GLA_KEY = D_MODEL // 2
GLA_VAL = D_MODEL
GLA_DK = GLA_KEY // GLA_HEADS
GLA_DV = GLA_VAL // GLA_HEADS
GATE_RANK = 16
GATE_TAU = 16.0
GLA_CHUNK = 64
GLA_SCALE = GLA_DK ** -0.5

N_KEYS = 128
N_EXPERTS = N_KEYS * N_KEYS
PEER_HEADS = 8
PEER_HALF = 128
PEER_TOPK = 16

LANES = 128
SUBLANES = 8
VMEM_LIMIT = 56 * 1024 * 1024

GLA_COL_Q = 0
GLA_COL_K = GLA_KEY
GLA_COL_V = 2 * GLA_KEY
GLA_COL_OG = 2 * GLA_KEY + GLA_VAL
GLA_COL_QM = 2 * GLA_KEY + 2 * GLA_VAL
GLA_COL_Z = GLA_COL_QM + MEM_WIDTH
GLA_Z_PAD = 512
GLA_IN_COLS = GLA_COL_Z + GLA_Z_PAD

BF16 = jnp.bfloat16
F32 = jnp.float32

_NT = (((1,), (1,)), ((), ()))
_TN = (((0,), (0,)), ((), ()))


def _params(*sem):
    return pltpu.CompilerParams(dimension_semantics=sem, vmem_limit_bytes=VMEM_LIMIT)


def _rmsnorm_kernel(x_ref, g_ref, *o_refs):
    x = x_ref[...]
    y = x * lax.rsqrt(jnp.mean(x * x, axis=-1, keepdims=True) + EPS) * g_ref[...]
    o_refs[0][...] = y.astype(o_refs[0].dtype)
    if len(o_refs) > 1:
        o_refs[1][...] = jnp.transpose(y).astype(o_refs[1].dtype)


def rmsnorm(x, g, out_dtype, *, tm, row0=0, rows=None, transposed=False):
    n, d = x.shape
    rows = n - row0 if rows is None else rows
    assert rows % tm == 0 and row0 % tm == 0
    blk0 = row0 // tm
    out_shape = [jax.ShapeDtypeStruct((rows, d), out_dtype)]
    out_specs = [pl.BlockSpec((tm, d), lambda i: (i, 0))]
    if transposed:
        out_shape.append(jax.ShapeDtypeStruct((d, rows), out_dtype))
        out_specs.append(pl.BlockSpec((d, tm), lambda i: (0, i)))
    res = pl.pallas_call(
        _rmsnorm_kernel,
        grid=(rows // tm,),
        in_specs=[pl.BlockSpec((tm, d), lambda i: (i + blk0, 0)),
                  pl.BlockSpec((1, d), lambda i: (0, 0))],
        out_specs=out_specs,
        out_shape=out_shape,
        compiler_params=_params("parallel"),
        name="rmsnorm",
    )(x, g.reshape(1, d))
    return res if transposed else res[0]


def _matmul_kernel(*refs, n_pairs, has_res):
    o_ref = refs[-1]
    acc = None
    for a_ref, w_ref in zip(refs[:n_pairs], refs[n_pairs:2 * n_pairs]):
        p = jnp.dot(a_ref[...], w_ref[...], preferred_element_type=F32)
        acc = p if acc is None else acc + p
    if has_res:
        acc = acc + refs[2 * n_pairs][...]
    o_ref[...] = acc.astype(o_ref.dtype)


def matmul(a_list, w_list, residual=None, *, tm, tn):
    n = a_list[0].shape[0]
    f = w_list[0].shape[1]
    assert n % tm == 0 and f % tn == 0
    in_specs = [pl.BlockSpec((tm, a.shape[1]), lambda i, j: (i, 0)) for a in a_list]
    in_specs += [pl.BlockSpec((w.shape[0], tn), lambda i, j: (0, j)) for w in w_list]
    args = list(a_list) + list(w_list)
    if residual is not None:
        in_specs.append(pl.BlockSpec((tm, tn), lambda i, j: (i, j)))
        args.append(residual)
    return pl.pallas_call(
        functools.partial(_matmul_kernel, n_pairs=len(a_list), has_res=residual is not None),
        grid=(n // tm, f // tn),
        in_specs=in_specs,
        out_specs=pl.BlockSpec((tm, tn), lambda i, j: (i, j)),
        out_shape=jax.ShapeDtypeStruct((n, f), F32),
        compiler_params=_params("parallel", "parallel"),
        name="matmul",
    )(*args)


def _pool_prompt_kernel(p_ref, wg_ref, sc_ref, o_ref):
    g = pl.program_id(1)
    x = p_ref[...]
    t_len = x.shape[0]
    row = lax.broadcasted_iota(jnp.int32, (t_len, 1), 0)

    def shifted(v, k):
        return jnp.where(row >= k, pltpu.roll(v, k, 0), 0.0)

    for gi, w in enumerate(POOL_WINDOWS):
        @pl.when(g == gi)
        def _(w=w):
            s = x
            k = 1
            while k < w:
                s = s + shifted(s, k)
                k *= 2
            cnt = jnp.minimum(w, row + 1).astype(F32)
            d = s / cnt - x
            out = jnp.dot(d.astype(BF16), wg_ref[0], preferred_element_type=F32)
            o_ref[...] = (out * sc_ref[0]).astype(o_ref.dtype)


def pool_prompt(u, w_group, scale):
    return pl.pallas_call(
        _pool_prompt_kernel,
        grid=(BATCH, POOL_GROUPS),
        in_specs=[pl.BlockSpec((SEQ, POOL_GC), lambda b, g: (b, g)),
                  pl.BlockSpec((1, POOL_GC, POOL_GC), lambda b, g: (g, 0, 0)),
                  pl.BlockSpec((1, 1, POOL_GC), lambda b, g: (g, 0, 0))],
        out_specs=pl.BlockSpec((SEQ, POOL_GC), lambda b, g: (b, g)),
        out_shape=jax.ShapeDtypeStruct((N_PROMPT, POOL_WIDTH), BF16),
        compiler_params=_params("parallel", "parallel"),
        name="pool_prompt",
    )(u, w_group.astype(BF16), scale.reshape(POOL_GROUPS, 1, POOL_GC))


def _pool_sample_kernel(prev_ref, p_ref, wg_ref, sc_ref, o_ref):
    g = pl.program_id(0)
    x = p_ref[...]
    for gi, w in enumerate(POOL_WINDOWS):
        @pl.when(g == gi)
        def _(w=w):
            s = x
            for r in range(POOL_BUF - (w - 1), POOL_BUF):
                s = s + prev_ref[r]
            cnt = float(min(w, PAST_LEN + 1))
            d = s / cnt - x
            out = jnp.dot(d.astype(BF16), wg_ref[0], preferred_element_type=F32)
            o_ref[...] = (out * sc_ref[0]).astype(o_ref.dtype)


def pool_sample(prev_t, p, w_group, scale):
    b = p.shape[0]
    return pl.pallas_call(
        _pool_sample_kernel,
        grid=(POOL_GROUPS,),
        in_specs=[pl.BlockSpec((POOL_BUF, b, POOL_GC), lambda g: (0, 0, g)),
                  pl.BlockSpec((b, POOL_GC), lambda g: (0, g)),
                  pl.BlockSpec((1, POOL_GC, POOL_GC), lambda g: (g, 0, 0)),
                  pl.BlockSpec((1, 1, POOL_GC), lambda g: (g, 0, 0))],
        out_specs=pl.BlockSpec((b, POOL_GC), lambda g: (0, g)),
        out_shape=jax.ShapeDtypeStruct((b, POOL_WIDTH), BF16),
        compiler_params=_params("parallel"),
        name="pool_sample",
    )(prev_t, p, w_group.astype(BF16), scale.reshape(POOL_GROUPS, 1, POOL_GC))


def _attn_prompt_kernel(q_ref, kv_ref, o_ref):
    for h in range(MEM_HEADS):
        lo, hi = h * MEM_HEAD_DIM, (h + 1) * MEM_HEAD_DIM
        qh = q_ref[:, lo:hi].astype(BF16)
        kh = kv_ref[:, lo:hi].astype(BF16)
        vh = kv_ref[:, MEM_WIDTH + lo:MEM_WIDTH + hi].astype(BF16)
        s = lax.dot_general(qh, kh, _NT, preferred_element_type=F32) * MEM_SCALE
        e = jnp.exp(s - jnp.max(s, axis=-1, keepdims=True))
        l = jnp.sum(e, axis=-1, keepdims=True)
        o = jnp.dot(e.astype(BF16), vh, preferred_element_type=F32) / l
        o_ref[:, lo:hi] = o.astype(o_ref.dtype)


def attn_prompt(u, q_col, kv, *, tq=512):
    assert q_col % MEM_WIDTH == 0
    nt = SEQ // tq
    return pl.pallas_call(
        _attn_prompt_kernel,
        grid=(BATCH, nt),
        in_specs=[pl.BlockSpec((tq, MEM_WIDTH), lambda b, t: (b * nt + t, q_col // MEM_WIDTH)),
                  pl.BlockSpec((MEM_TOKENS, 2 * MEM_WIDTH), lambda b, t: (b, 0))],
        out_specs=pl.BlockSpec((tq, MEM_WIDTH), lambda b, t: (b * nt + t, 0)),
        out_shape=jax.ShapeDtypeStruct((N_PROMPT, MEM_WIDTH), BF16),
        compiler_params=_params("parallel", "parallel"),
        name="attn_prompt",
    )(u, kv)


def _attn_sample_kernel(q_ref, k_ref, v_ref, o_ref):
    for i in range(q_ref.shape[0]):
        prod = k_ref[i] * q_ref[i]
        vb = v_ref[i]
        for h in range(MEM_HEADS):
            lo, hi = h * MEM_HEAD_DIM, (h + 1) * MEM_HEAD_DIM
            s = jnp.sum(prod[:, lo:hi], axis=-1, keepdims=True) * MEM_SCALE
            e = jnp.exp(s - jnp.max(s, axis=0, keepdims=True))
            p = e / jnp.sum(e, axis=0, keepdims=True)
            o = jnp.sum(p * vb[:, lo:hi], axis=0, keepdims=True)
            o_ref[i, :, lo:hi] = o.astype(o_ref.dtype)


def attn_sample(q, mem_k, mem_v, *, bb=4):
    b = q.shape[0]
    return pl.pallas_call(
        _attn_sample_kernel,
        grid=(b // bb,),
        in_specs=[pl.BlockSpec((bb, 1, MEM_WIDTH), lambda i: (i, 0, 0)),
                  pl.BlockSpec((bb, MEM_TOKENS, MEM_WIDTH), lambda i: (i, 0, 0)),
                  pl.BlockSpec((bb, MEM_TOKENS, MEM_WIDTH), lambda i: (i, 0, 0))],
        out_specs=pl.BlockSpec((bb, 1, MEM_WIDTH), lambda i: (i, 0, 0)),
        out_shape=jax.ShapeDtypeStruct((b, 1, MEM_WIDTH), BF16),
        compiler_params=_params("parallel"),
        name="attn_sample",
    )(q, mem_k, mem_v)


def _log_decay(z, wg_ref, bg_ref):
    g = jnp.dot(z.astype(BF16), wg_ref[...], preferred_element_type=F32) + bg_ref[...]
    return jax.nn.log_sigmoid(g) / GATE_TAU


def _gla_out(o, og, gn):
    on = o * lax.rsqrt(jnp.mean(o * o, axis=-1, keepdims=True) + EPS) * gn
    return on * (og * jax.nn.sigmoid(og))


def _gla_prompt_kernel(q_ref, k_ref, v_ref, og_ref, z_ref, wg_ref, bg_ref, gn_ref,
                       mix_ref, st_ref, s_ref):
    n = pl.program_id(1)
    c = q_ref.shape[0]

    @pl.when(n == 0)
    def _():
        s_ref[...] = jnp.zeros_like(s_ref)

    la = _log_decay(z_ref[...], wg_ref, bg_ref)
    row = lax.broadcasted_iota(jnp.int32, (c, 1), 0)
    b = la
    k = 1
    while k < c:
        b = b + jnp.where(row >= k, pltpu.roll(b, k, 0), 0.0)
        k *= 2
    b_last = b[c - 1:c, :]
    q_in = q_ref[...] * jnp.exp(b) * GLA_SCALE
    k_in = k_ref[...] * jnp.exp(-b)
    k_out = k_ref[...] * jnp.exp(b_last - b)
    decay = jnp.exp(b_last)
    causal = row >= lax.broadcasted_iota(jnp.int32, (1, c), 1)
    gn = gn_ref[...]
    for h in range(GLA_HEADS):
        klo, khi = h * GLA_DK, (h + 1) * GLA_DK
        vlo, vhi = h * GLA_DV, (h + 1) * GLA_DV
        qh = q_in[:, klo:khi].astype(BF16)
        vh = v_ref[:, vlo:vhi].astype(BF16)
        a = lax.dot_general(qh, k_in[:, klo:khi].astype(BF16), _NT, preferred_element_type=F32)
        a = jnp.where(causal, a, 0.0)
        st = s_ref[h]
        o = (jnp.dot(a.astype(BF16), vh, preferred_element_type=F32)
             + lax.dot_general(qh, st.astype(BF16), _NT, preferred_element_type=F32))
        s_ref[h] = st * decay[:, klo:khi] + lax.dot_general(
            vh, k_out[:, klo:khi].astype(BF16), _TN, preferred_element_type=F32)
        mix_ref[:, vlo:vhi] = _gla_out(o, og_ref[:, vlo:vhi], gn).astype(mix_ref.dtype)

    @pl.when(n == pl.num_programs(1) - 1)
    def _():
        st_ref[0] = s_ref[...]


def gla_prompt(u, wg_pad, b_gate, norm_g):
    nc = SEQ // GLA_CHUNK
    c = GLA_CHUNK

    def col(width, off):
        assert off % width == 0
        return pl.BlockSpec((c, width), lambda b, n: (b * nc + n, off // width))

    return pl.pallas_call(
        _gla_prompt_kernel,
        grid=(BATCH, nc),
        in_specs=[col(GLA_KEY, GLA_COL_Q), col(GLA_KEY, GLA_COL_K), col(GLA_VAL, GLA_COL_V),
                  col(GLA_VAL, GLA_COL_OG), col(LANES, GLA_COL_Z),
                  pl.BlockSpec((LANES, GLA_KEY), lambda b, n: (0, 0)),
                  pl.BlockSpec((1, GLA_KEY), lambda b, n: (0, 0)),
                  pl.BlockSpec((1, GLA_DV), lambda b, n: (0, 0))],
        out_specs=[pl.BlockSpec((c, GLA_VAL), lambda b, n: (b * nc + n, 0)),
                   pl.BlockSpec((1, GLA_HEADS, GLA_DV, GLA_DK), lambda b, n: (b, 0, 0, 0))],
        out_shape=[jax.ShapeDtypeStruct((N_PROMPT, GLA_VAL), BF16),
                   jax.ShapeDtypeStruct((BATCH, GLA_HEADS, GLA_DV, GLA_DK), F32)],
        scratch_shapes=[pltpu.VMEM((GLA_HEADS, GLA_DV, GLA_DK), F32)],
        compiler_params=_params("parallel", "arbitrary"),
        name="gla_prompt",
    )(u, u, u, u, u, wg_pad, b_gate.reshape(1, GLA_KEY), norm_g.reshape(1, GLA_DV))


def _gla_sample_kernel(us_ref, s0_ref, wg_ref, bg_ref, gn_ref, mix_ref, s1_ref, la_ref):
    i = pl.program_id(0)

    @pl.when(i == 0)
    def _():
        la_ref[...] = _log_decay(us_ref[:, GLA_COL_Z:GLA_COL_Z + LANES], wg_ref, bg_ref)

    tok = pl.ds(i, 1)
    gn = gn_ref[...]
    for h in range(GLA_HEADS):
        klo, khi = h * GLA_DK, (h + 1) * GLA_DK
        vlo, vhi = h * GLA_DV, (h + 1) * GLA_DV
        la = la_ref[tok, klo:khi]
        q = us_ref[tok, GLA_COL_Q + klo:GLA_COL_Q + khi]
        k = us_ref[tok, GLA_COL_K + klo:GLA_COL_K + khi]
        v = us_ref[tok, GLA_COL_V + vlo:GLA_COL_V + vhi]
        og = us_ref[tok, GLA_COL_OG + vlo:GLA_COL_OG + vhi]
        q_in = q * jnp.exp(la) * GLA_SCALE
        k_in = k * jnp.exp(-la)
        k_out = k * jnp.exp(la - la)
        decay = jnp.exp(la)
        a = jnp.sum(q_in * k_in, axis=-1, keepdims=True)
        rows = jnp.concatenate(
            [q_in, k_out, decay, jnp.zeros((SUBLANES - 3, GLA_DK), F32)], axis=0)
        cols = jnp.transpose(rows)
        s0 = s0_ref[0, h]
        o = a * v + jnp.sum(cols[:, 0:1] * s0, axis=0, keepdims=True)
        s1_ref[0, h] = s0 * cols[:, 2:3] + cols[:, 1:2] * v
        mix_ref[tok, vlo:vhi] = _gla_out(o, og, gn).astype(mix_ref.dtype)


def gla_sample(us, s0, wg_pad, b_gate, norm_g):
    b = us.shape[0]
    return pl.pallas_call(
        _gla_sample_kernel,
        grid=(b,),
        in_specs=[pl.BlockSpec((b, GLA_IN_COLS), lambda i: (0, 0)),
                  pl.BlockSpec((1, GLA_HEADS, GLA_DK, GLA_DV), lambda i: (i, 0, 0, 0)),
                  pl.BlockSpec((LANES, GLA_KEY), lambda i: (0, 0)),
                  pl.BlockSpec((1, GLA_KEY), lambda i: (0, 0)),
                  pl.BlockSpec((1, GLA_DV), lambda i: (0, 0))],
        out_specs=[pl.BlockSpec((b, GLA_VAL), lambda i: (0, 0)),
                   pl.BlockSpec((1, GLA_HEADS, GLA_DK, GLA_DV), lambda i: (i, 0, 0, 0))],
        out_shape=[jax.ShapeDtypeStruct((b, GLA_VAL), F32),
                   jax.ShapeDtypeStruct(s0.shape, F32)],
        scratch_shapes=[pltpu.VMEM((b, GLA_KEY), F32)],
        compiler_params=_params("arbitrary"),
        name="gla_sample",
    )(us, s0, wg_pad, b_gate.reshape(1, GLA_KEY), norm_g.reshape(1, GLA_DV))


def _peer_score_kernel(q_ref, keys_ref, o_ref):
    for hp in range(2 * PEER_HEADS):
        qs = q_ref[:, hp * PEER_HALF:(hp + 1) * PEER_HALF].astype(BF16)
        o_ref[hp] = lax.dot_general(keys_ref[hp], qs, _NT, preferred_element_type=F32)


def peer_scores(q, keys_bf16, *, tn=640):
    n = q.shape[0]
    assert n % tn == 0
    return pl.pallas_call(
        _peer_score_kernel,
        grid=(n // tn,),
        in_specs=[pl.BlockSpec((tn, q.shape[1]), lambda i: (i, 0)),
                  pl.BlockSpec((2 * PEER_HEADS, N_KEYS, PEER_HALF), lambda i: (0, 0, 0))],
        out_specs=pl.BlockSpec((2 * PEER_HEADS, N_KEYS, tn), lambda i: (0, 0, i)),
        out_shape=jax.ShapeDtypeStruct((2 * PEER_HEADS, N_KEYS, n), F32),
        compiler_params=_params("parallel"),
        name="peer_scores",
    )(q, keys_bf16)


_PEER_CAND = tuple((a, b) for a in range(PEER_TOPK) for b in range(PEER_TOPK)
                   if (a + 1) * (b + 1) <= PEER_TOPK)
_NEG_INF = float("-inf")
_UNRANKED = float(N_KEYS)


_REDUCE_WIDTH = 8


def _reduce(op, vals):
    accs = list(vals[:_REDUCE_WIDTH])
    for n, v in enumerate(vals[_REDUCE_WIDTH:]):
        accs[n % _REDUCE_WIDTH] = op(accs[n % _REDUCE_WIDTH], v)
    while len(accs) > 1:
        accs = [op(accs[n], accs[n + 1]) if n + 1 < len(accs) else accs[n]
                for n in range(0, len(accs), 2)]
    return accs[0]


def _peer_select_kernel(s_ref, e0_ref, li_ref, e1_ref, rb_ref,
                        cur_ref, sv_ref, ix_ref, cand_ref, cnt_ref):
    tile = s_ref.shape[2:]

    for p in range(2):
        cur_ref[...] = s_ref[p]
        if p == 1:
            rb_ref[0] = jnp.full((N_KEYS,) + tile, _UNRANKED, F32)

        def extract(r, m, p=p):
            idx = _reduce(jnp.minimum, [jnp.where(cur_ref[k] == m, float(k), _UNRANKED)
                                        for k in range(N_KEYS)])
            sv_ref[p, pl.ds(r, 1)] = m[None]
            if p == 0:
                ix_ref[pl.ds(r, 1)] = idx[None]
            rf = lax.convert_element_type(r, F32)
            rest = []
            for k in range(N_KEYS):
                hit = idx == float(k)
                c = jnp.where(hit, _NEG_INF, cur_ref[k])
                cur_ref[k] = c
                if p == 1:
                    rb_ref[0, k] = jnp.where(hit, rf, rb_ref[0, k])
                rest.append(c)
            return _reduce(jnp.maximum, rest)

        lax.fori_loop(0, PEER_TOPK, extract,
                      _reduce(jnp.maximum, [s_ref[p, k] for k in range(N_KEYS)]))

    for ci, (a, b) in enumerate(_PEER_CAND):
        cand_ref[ci] = sv_ref[0, a] + sv_ref[1, b]
    cnt_ref[...] = jnp.zeros_like(cnt_ref)
    c_max = sv_ref[0, 0] + sv_ref[1, 0]
    no_flat = float(PEER_TOPK * PEER_TOPK)

    def pick(_, z):
        m = _reduce(jnp.maximum, [cand_ref[ci] for ci in range(len(_PEER_CAND))])
        flat = _reduce(jnp.minimum, [
            jnp.where(cand_ref[ci] == m, float(a * PEER_TOPK + b), no_flat)
            for ci, (a, b) in enumerate(_PEER_CAND)])
        for ci, (a, b) in enumerate(_PEER_CAND):
            cand_ref[ci] = jnp.where(flat == float(a * PEER_TOPK + b), _NEG_INF, cand_ref[ci])
        first = jnp.floor(flat * (1.0 / PEER_TOPK))
        for a in range(PEER_TOPK):
            cnt_ref[a] = cnt_ref[a] + jnp.where(first == float(a), 1.0, 0.0)
        return z + jnp.exp(m - c_max)

    inv_z = 1.0 / lax.fori_loop(0, PEER_TOPK, pick, jnp.zeros(tile, F32))

    max0 = sv_ref[0, 0]
    max1 = sv_ref[1, 0]

    def emit(k, carry):
        kf = lax.convert_element_type(k, F32)
        li = jnp.zeros(tile, F32)
        for a in range(PEER_TOPK):
            li = jnp.where(ix_ref[a] == kf, cnt_ref[a], li)
        li_ref[0, k] = li
        e0_ref[0, k] = jnp.exp(s_ref[0, k] - max0) * inv_z
        e1_ref[0, k] = jnp.exp(s_ref[1, k] - max1)
        return carry

    lax.fori_loop(0, N_KEYS, emit, 0, unroll=4)


def peer_select(s4, *, rows, rblk):
    assert rows % rblk == 0 and (rblk % SUBLANES == 0 or rblk == s4.shape[2])
    tile = (rblk, LANES)
    out = jax.ShapeDtypeStruct((PEER_HEADS, N_KEYS, rows, LANES), F32)
    out_spec = pl.BlockSpec((1, N_KEYS) + tile, lambda h, r: (h, 0, r, 0))
    res = pl.pallas_call(
        _peer_select_kernel,
        grid=(PEER_HEADS, rows // rblk),
        in_specs=[pl.BlockSpec((2, N_KEYS) + tile, lambda h, r: (h, 0, r, 0))],
        out_specs=[out_spec] * 4,
        out_shape=[out] * 4,
        scratch_shapes=[pltpu.VMEM((N_KEYS,) + tile, F32),
                        pltpu.VMEM((2, PEER_TOPK) + tile, F32),
                        pltpu.VMEM((PEER_TOPK,) + tile, F32),
                        pltpu.VMEM((len(_PEER_CAND),) + tile, F32),
                        pltpu.VMEM((PEER_TOPK,) + tile, F32)],
        compiler_params=_params("parallel", "parallel"),
        name="peer_select",
    )(s4)
    return [r.reshape(PEER_HEADS, N_KEYS, rows * LANES) for r in res]


_DENSE_CHUNK_KEYS = 32
_DENSE_GROUP_KEYS1 = 4
_DENSE_ROWS_PER_DOT = 512


def _gelu(x):
    return 0.5 * x * (1.0 + lax.erf(x * (2.0 ** -0.5)))


def _peer_dense_stages(ht_ref, u_ref, vt_ref, e0_ref, li_ref, e1_ref, rb_ref, o_ref,
                       a_wr, a_rd, w_wr, w_rd, *, ib):
    tn = ht_ref.shape[1]

    for r0 in range(0, ib * N_KEYS, _DENSE_ROWS_PER_DOT):
        rs = slice(r0, r0 + _DENSE_ROWS_PER_DOT)
        a_wr[rs, :] = jnp.dot(u_ref[rs, :], ht_ref[...], preferred_element_type=F32)

    jc = _DENSE_CHUNK_KEYS
    for i0 in range(0, ib, _DENSE_GROUP_KEYS1):
        iis = range(i0, i0 + _DENSE_GROUP_KEYS1)
        for j0 in range(0, N_KEYS, jc):
            for c0 in range(0, tn, LANES):
                lanes = slice(c0, c0 + LANES)
                js = slice(j0, j0 + jc)
                g = [None] * len(iis)
                for h in range(PEER_HEADS):
                    rb = rb_ref[h, js, lanes]
                    e1 = e1_ref[h, js, lanes]
                    for n, ii in enumerate(iis):
                        e0 = e0_ref[h, ii:ii + 1, lanes]
                        li = li_ref[h, ii:ii + 1, lanes]
                        t = jnp.where(rb < li, e0 * e1, 0.0)
                        g[n] = t if g[n] is None else g[n] + t
                for n, ii in enumerate(iis):
                    rows = slice(ii * N_KEYS + j0, ii * N_KEYS + j0 + jc)
                    w_wr[rows, lanes] = (_gelu(a_rd[rows, lanes]) * g[n]).astype(w_wr.dtype)

    for r0 in range(0, vt_ref.shape[0], 2 * _DENSE_ROWS_PER_DOT):
        rs = slice(r0, r0 + 2 * _DENSE_ROWS_PER_DOT)
        o_ref[rs, :] += jnp.dot(vt_ref[rs, :], w_rd[...], preferred_element_type=F32)


def _peer_dense_kernel(ht_ref, u_ref, vt_ref, e0_ref, li_ref, e1_ref, rb_ref, o_ref,
                       a0_ref, a1_ref, w0_ref, w1_ref, *, ib, n_eb):
    s = pl.program_id(0)

    @pl.when(s == 0)
    def _():
        for ref in (a0_ref, a1_ref, w0_ref, w1_ref):
            ref[...] = jnp.zeros_like(ref)

    @pl.when((s < 2) | ((s - 2) % n_eb == 0))
    def _():
        o_ref[...] = jnp.zeros_like(o_ref)

    io = (ht_ref, u_ref, vt_ref, e0_ref, li_ref, e1_ref, rb_ref, o_ref)

    @pl.when(s % 2 == 0)
    def _():
        _peer_dense_stages(*io, a0_ref, a1_ref, w1_ref, w0_ref, ib=ib)

    @pl.when(s % 2 == 1)
    def _():
        _peer_dense_stages(*io, a1_ref, a0_ref, w0_ref, w1_ref, ib=ib)


def peer_dense(h_t, u_bf16, vt_bf16, e0, li, e1, rb, *, tok0, tn, ib=8):
    d = h_t.shape[0]
    ntok = e0.shape[-1]
    ebs = ib * N_KEYS
    assert ntok % tn == 0 and tok0 % tn == 0 and N_KEYS % ib == 0 and ib % _DENSE_GROUP_KEYS1 == 0
    t0 = tok0 // tn
    n_eb = N_KEYS // ib
    total = (ntok // tn) * n_eb

    def stage(lag):
        return lambda s: jnp.clip(s - lag, 0, total - 1)

    fa, fb, fc = stage(0), stage(1), stage(2)
    return pl.pallas_call(
        functools.partial(_peer_dense_kernel, ib=ib, n_eb=n_eb),
        grid=(total + 2,),
        in_specs=[pl.BlockSpec((d, tn), lambda s: (0, fa(s) // n_eb + t0)),
                  pl.BlockSpec((ebs, d), lambda s: (fa(s) % n_eb, 0)),
                  pl.BlockSpec((d, ebs), lambda s: (0, fc(s) % n_eb)),
                  pl.BlockSpec((PEER_HEADS, ib, tn), lambda s: (0, fb(s) % n_eb, fb(s) // n_eb)),
                  pl.BlockSpec((PEER_HEADS, ib, tn), lambda s: (0, fb(s) % n_eb, fb(s) // n_eb)),
                  pl.BlockSpec((PEER_HEADS, N_KEYS, tn), lambda s: (0, 0, fb(s) // n_eb)),
                  pl.BlockSpec((PEER_HEADS, N_KEYS, tn), lambda s: (0, 0, fb(s) // n_eb))],
        out_specs=pl.BlockSpec((d, tn), lambda s: (0, fc(s) // n_eb)),
        out_shape=jax.ShapeDtypeStruct((d, ntok), F32),
        scratch_shapes=[pltpu.VMEM((ebs, tn), F32), pltpu.VMEM((ebs, tn), F32),
                        pltpu.VMEM((ebs, tn), BF16), pltpu.VMEM((ebs, tn), BF16)],
        compiler_params=_params("arbitrary"),
        name="peer_dense",
    )(h_t, u_bf16, vt_bf16, e0, li, e1, rb)


def _table_kernel(x_ref, o_ref, *, transposed):
    x = x_ref[...]
    o_ref[...] = (jnp.transpose(x) if transposed else x).astype(o_ref.dtype)


def expert_table_bf16(tabs, layer, *, transposed, rows=512):
    _, e, d = tabs.shape
    assert e % rows == 0
    if transposed:
        out_shape, out_spec = (d, e), pl.BlockSpec((d, rows), lambda i: (0, i))
    else:
        out_shape, out_spec = (e, d), pl.BlockSpec((rows, d), lambda i: (i, 0))
    return pl.pallas_call(
        functools.partial(_table_kernel, transposed=transposed),
        grid=(e // rows,),
        in_specs=[pl.BlockSpec((None, rows, d), lambda i: (layer, i, 0))],
        out_specs=out_spec,
        out_shape=jax.ShapeDtypeStruct(out_shape, BF16),
        compiler_params=_params("parallel"),
        name="expert_table",
    )(tabs)


def _add_transposed_kernel(x_ref, yt_ref, o_ref):
    o_ref[...] = x_ref[...] + jnp.transpose(yt_ref[...])


def add_transposed(x, y_t, *, tm):
    n, d = x.shape
    return pl.pallas_call(
        _add_transposed_kernel,
        grid=(n // tm,),
        in_specs=[pl.BlockSpec((tm, d), lambda i: (i, 0)), pl.BlockSpec((d, tm), lambda i: (0, i))],
        out_specs=pl.BlockSpec((tm, d), lambda i: (i, 0)),
        out_shape=jax.ShapeDtypeStruct((n, d), F32),
        compiler_params=_params("parallel"),
        name="add_transposed",
    )(x, y_t)


def peer_layer(x, norm_g, w_q, keys, u_tabs, v_tabs, layer):
    h, h_t = rmsnorm(x, norm_g, BF16, tm=640, transposed=True)
    q = matmul([h], [w_q.astype(BF16)], tm=1664, tn=512)
    s_t = peer_scores(q, keys.reshape(2 * PEER_HEADS, N_KEYS, PEER_HALF).astype(BF16))
    sel_p = peer_select(s_t.reshape(2 * PEER_HEADS, N_KEYS, N_TOK // LANES, LANES),
                        rows=N_PROMPT // LANES, rblk=32)
    sel_s = peer_select(s_t[:, :, N_PROMPT:].reshape(2 * PEER_HEADS, N_KEYS, 1, DEC_BATCH),
                        rows=1, rblk=1)
    u_bf16 = expert_table_bf16(u_tabs, layer, transposed=False)
    vt_bf16 = expert_table_bf16(v_tabs, layer, transposed=True)
    out_p = peer_dense(h_t, u_bf16, vt_bf16, *sel_p, tok0=0, tn=512)
    out_s = peer_dense(h_t, u_bf16, vt_bf16, *sel_s, tok0=N_PROMPT, tn=DEC_BATCH)
    return add_transposed(x, jnp.concatenate([out_p, out_s], axis=1), tm=640)


def kernel(x_prompt, x_sample, cache_mem_k, cache_mem_v, state_pool, state_gla, mem_prompt,
           norm_mix, norm_ffn, norm_final, mem_norm, w_mem_kv,
           pool_w_in, pool_w_group, pool_scale, pool_w_out,
           gla_w_in, gla_w_gate, gla_b_gate, gla_norm, gla_w_out,
           peer_w_q, peer_keys, peer_u, peer_v):
    d = D_MODEL
    x = jnp.concatenate([x_prompt.reshape(N_PROMPT, d), x_sample.reshape(DEC_BATCH, d)], axis=0)

    mem = mem_prompt.reshape(BATCH * MEM_TOKENS, d)
    kvs = []
    for i in range(DEPTH):
        mn = rmsnorm(mem, mem_norm[i], BF16, tm=512)
        kvs.append(matmul([mn], [w_mem_kv[i].astype(BF16)], tm=512, tn=512))

    def split_kv(kv, lo):
        return kv[:, lo:lo + MEM_WIDTH].reshape(BATCH, MEM_TOKENS, MEM_HEADS, MEM_HEAD_DIM)

    mem_k_prompt = jnp.stack([split_kv(kv, 0) for kv in kvs])
    mem_v_prompt = jnp.stack([split_kv(kv, MEM_WIDTH) for kv in kvs])

    def cache(c, i):
        return c[i].reshape(DEC_BATCH, MEM_TOKENS, MEM_WIDTH)

    h = rmsnorm(x, norm_mix[0], BF16, tm=640)
    u = matmul([h], [pool_w_in[0].astype(BF16)], tm=1664, tn=512)
    us = u[N_PROMPT:]
    p_s = us[:, :POOL_WIDTH]
    mix = jnp.concatenate([
        pool_prompt(u, pool_w_group[0], pool_scale[0]),
        pool_sample(jnp.transpose(state_pool[0], (1, 0, 2)), p_s, pool_w_group[0], pool_scale[0]),
    ], axis=0)
    att = jnp.concatenate([
        attn_prompt(u, POOL_WIDTH, kvs[0]),
        attn_sample(us[:, POOL_WIDTH:].reshape(DEC_BATCH, 1, MEM_WIDTH),
                    cache(cache_mem_k, 0), cache(cache_mem_v, 0)).reshape(DEC_BATCH, MEM_WIDTH),
    ], axis=0)
    w_out = pool_w_out[0].astype(BF16)
    x = matmul([mix, att], [w_out[:POOL_WIDTH], w_out[POOL_WIDTH:]], x, tm=1664, tn=512)
    pool_prompt_state = u[:N_PROMPT, :POOL_WIDTH].reshape(BATCH, SEQ, POOL_WIDTH)[:, SEQ - POOL_BUF:][None]
    pool_sample_state = jnp.concatenate([state_pool[0][:, 1:], p_s[:, None, :]], axis=1)[None]
    x = peer_layer(x, norm_ffn[0], peer_w_q[0], peer_keys[0], peer_u, peer_v, 0)

    w_in = gla_w_in[0]
    o3 = 2 * GLA_KEY + GLA_VAL
    o4 = o3 + GATE_RANK
    o5 = o4 + GLA_VAL
    w_in = jnp.concatenate([
        w_in[:, :o3], w_in[:, o4:o5], w_in[:, o5:], w_in[:, o3:o4],
        jnp.zeros((d, GLA_Z_PAD - GATE_RANK), w_in.dtype)], axis=1).astype(BF16)
    wg_pad = jnp.concatenate(
        [gla_w_gate[0], jnp.zeros((LANES - GATE_RANK, GLA_KEY), gla_w_gate.dtype)], axis=0).astype(BF16)
    h = rmsnorm(x, norm_mix[1], BF16, tm=640)
    u = matmul([h], [w_in], tm=1664, tn=512)
    us = u[N_PROMPT:]
    mix_p, st_t = gla_prompt(u, wg_pad, gla_b_gate[0], gla_norm[0])
    mix_s, gla_s = gla_sample(us, state_gla[0], wg_pad, gla_b_gate[0], gla_norm[0])
    mix = jnp.concatenate([mix_p, mix_s.astype(BF16)], axis=0)
    att = jnp.concatenate([
        attn_prompt(u, GLA_COL_QM, kvs[1]),
        attn_sample(us[:, GLA_COL_QM:GLA_COL_QM + MEM_WIDTH].reshape(DEC_BATCH, 1, MEM_WIDTH),
                    cache(cache_mem_k, 1), cache(cache_mem_v, 1)).reshape(DEC_BATCH, MEM_WIDTH),
    ], axis=0)
    w_out = gla_w_out[0].astype(BF16)
    x = matmul([mix, att], [w_out[:GLA_VAL], w_out[GLA_VAL:]], x, tm=1664, tn=512)
    x = peer_layer(x, norm_ffn[1], peer_w_q[1], peer_keys[1], peer_u, peer_v, 1)

    y_prompt = rmsnorm(x, norm_final, F32, tm=512, row0=0, rows=N_PROMPT)
    y_sample = rmsnorm(x, norm_final, F32, tm=DEC_BATCH, row0=N_PROMPT, rows=DEC_BATCH)
    return (y_prompt.reshape(BATCH, SEQ, d),
            y_sample.reshape(DEC_BATCH, 1, d),
            pool_prompt_state,
            jnp.swapaxes(st_t, -1, -2)[None],
            mem_k_prompt,
            mem_v_prompt,
            pool_sample_state,
            gla_s[None])
```

```python
import functools

import jax
import jax.numpy as jnp
from jax import lax
from jax.experimental import pallas as pl
from jax.experimental.pallas import tpu as pltpu

D_MODEL = 2048
BATCH = 4
SEQ = 2048
DEPTH = 2
DEC_BATCH = 128
PAST_LEN = 16384
EPS = 1e-6

N_PROMPT = BATCH * SEQ
N_TOK = N_PROMPT + DEC_BATCH

POOL_WIDTH = D_MODEL // 2
POOL_GROUPS = 4
POOL_GC = POOL_WIDTH // POOL_GROUPS
POOL_WINDOWS = (2, 4, 8, 16)
POOL_BUF = max(POOL_WINDOWS) - 1

MEM_TOKENS = 256
MEM_HEADS = 4
MEM_HEAD_DIM = D_MODEL // 8
MEM_WIDTH = MEM_HEADS * MEM_HEAD_DIM
MEM_SCALE = MEM_HEAD_DIM ** -0.5

GLA_HEADS = 4
GLA_KEY = D_MODEL // 2
GLA_VAL = D_MODEL
GLA_DK = GLA_KEY // GLA_HEADS
GLA_DV = GLA_VAL // GLA_HEADS
GATE_RANK = 16
GATE_TAU = 16.0
GLA_CHUNK = 64
GLA_SCALE = GLA_DK ** -0.5

N_KEYS = 128
N_EXPERTS = N_KEYS * N_KEYS
PEER_HEADS = 8
PEER_HALF = 128
PEER_TOPK = 16

LANES = 128
SUBLANES = 8
VMEM_LIMIT = 56 * 1024 * 1024

GLA_COL_Q = 0
GLA_COL_K = GLA_KEY
GLA_COL_V = 2 * GLA_KEY
GLA_COL_OG = 2 * GLA_KEY + GLA_VAL
GLA_COL_QM = 2 * GLA_KEY + 2 * GLA_VAL
GLA_COL_Z = GLA_COL_QM + MEM_WIDTH
GLA_Z_PAD = 512
GLA_IN_COLS = GLA_COL_Z + GLA_Z_PAD

BF16 = jnp.bfloat16
F32 = jnp.float32

_NT = (((1,), (1,)), ((), ()))
_TN = (((0,), (0,)), ((), ()))


def _params(*sem):
    return pltpu.CompilerParams(dimension_semantics=sem, vmem_limit_bytes=VMEM_LIMIT)


def _rmsnorm_kernel(x_ref, g_ref, *o_refs):
    x = x_ref[...]
    y = x * lax.rsqrt(jnp.mean(x * x, axis=-1, keepdims=True) + EPS) * g_ref[...]
    o_refs[0][...] = y.astype(o_refs[0].dtype)
    if len(o_refs) > 1:
        o_refs[1][...] = jnp.transpose(y).astype(o_refs[1].dtype)


def rmsnorm(x, g, out_dtype, *, tm, row0=0, rows=None, transposed=False):
    n, d = x.shape
    rows = n - row0 if rows is None else rows
    assert rows % tm == 0 and row0 % tm == 0
    blk0 = row0 // tm
    out_shape = [jax.ShapeDtypeStruct((rows, d), out_dtype)]
    out_specs = [pl.BlockSpec((tm, d), lambda i: (i, 0))]
    if transposed:
        out_shape.append(jax.ShapeDtypeStruct((d, rows), out_dtype))
        out_specs.append(pl.BlockSpec((d, tm), lambda i: (0, i)))
    res = pl.pallas_call(
        _rmsnorm_kernel,
        grid=(rows // tm,),
        in_specs=[pl.BlockSpec((tm, d), lambda i: (i + blk0, 0)),
                  pl.BlockSpec((1, d), lambda i: (0, 0))],
        out_specs=out_specs,
        out_shape=out_shape,
        compiler_params=_params("parallel"),
        name="rmsnorm",
    )(x, g.reshape(1, d))
    return res if transposed else res[0]


def _matmul_kernel(*refs, n_pairs, has_res):
    o_ref = refs[-1]
    acc = None
    for a_ref, w_ref in zip(refs[:n_pairs], refs[n_pairs:2 * n_pairs]):
        p = jnp.dot(a_ref[...], w_ref[...], preferred_element_type=F32)
        acc = p if acc is None else acc + p
    if has_res:
        acc = acc + refs[2 * n_pairs][...]
    o_ref[...] = acc.astype(o_ref.dtype)


def matmul(a_list, w_list, residual=None, *, tm, tn):
    n = a_list[0].shape[0]
    f = w_list[0].shape[1]
    assert n % tm == 0 and f % tn == 0
    in_specs = [pl.BlockSpec((tm, a.shape[1]), lambda i, j: (i, 0)) for a in a_list]
    in_specs += [pl.BlockSpec((w.shape[0], tn), lambda i, j: (0, j)) for w in w_list]
    args = list(a_list) + list(w_list)
    if residual is not None:
        in_specs.append(pl.BlockSpec((tm, tn), lambda i, j: (i, j)))
        args.append(residual)
    return pl.pallas_call(
        functools.partial(_matmul_kernel, n_pairs=len(a_list), has_res=residual is not None),
        grid=(n // tm, f // tn),
        in_specs=in_specs,
        out_specs=pl.BlockSpec((tm, tn), lambda i, j: (i, j)),
        out_shape=jax.ShapeDtypeStruct((n, f), F32),
        compiler_params=_params("parallel", "parallel"),
        name="matmul",
    )(*args)


def _pool_prompt_kernel(p_ref, wg_ref, sc_ref, o_ref):
    g = pl.program_id(1)
    x = p_ref[...]
    t_len = x.shape[0]
    row = lax.broadcasted_iota(jnp.int32, (t_len, 1), 0)

    def shifted(v, k):
        return jnp.where(row >= k, pltpu.roll(v, k, 0), 0.0)

    for gi, w in enumerate(POOL_WINDOWS):
        @pl.when(g == gi)
        def _(w=w):
            s = x
            k = 1
            while k < w:
                s = s + shifted(s, k)
                k *= 2
            cnt = jnp.minimum(w, row + 1).astype(F32)
            d = s / cnt - x
            out = jnp.dot(d.astype(BF16), wg_ref[0], preferred_element_type=F32)
            o_ref[...] = (out * sc_ref[0]).astype(o_ref.dtype)


def pool_prompt(u, w_group, scale):
    return pl.pallas_call(
        _pool_prompt_kernel,
        grid=(BATCH, POOL_GROUPS),
        in_specs=[pl.BlockSpec((SEQ, POOL_GC), lambda b, g: (b, g)),
                  pl.BlockSpec((1, POOL_GC, POOL_GC), lambda b, g: (g, 0, 0)),
                  pl.BlockSpec((1, 1, POOL_GC), lambda b, g: (g, 0, 0))],
        out_specs=pl.BlockSpec((SEQ, POOL_GC), lambda b, g: (b, g)),
        out_shape=jax.ShapeDtypeStruct((N_PROMPT, POOL_WIDTH), BF16),
        compiler_params=_params("parallel", "parallel"),
        name="pool_prompt",
    )(u, w_group.astype(BF16), scale.reshape(POOL_GROUPS, 1, POOL_GC))


def _pool_sample_kernel(prev_ref, p_ref, wg_ref, sc_ref, o_ref):
    g = pl.program_id(0)
    x = p_ref[...]
    for gi, w in enumerate(POOL_WINDOWS):
        @pl.when(g == gi)
        def _(w=w):
            s = x
            for r in range(POOL_BUF - (w - 1), POOL_BUF):
                s = s + prev_ref[r]
            cnt = float(min(w, PAST_LEN + 1))
            d = s / cnt - x
            out = jnp.dot(d.astype(BF16), wg_ref[0], preferred_element_type=F32)
            o_ref[...] = (out * sc_ref[0]).astype(o_ref.dtype)


def pool_sample(prev_t, p, w_group, scale):
    b = p.shape[0]
    return pl.pallas_call(
        _pool_sample_kernel,
        grid=(POOL_GROUPS,),
        in_specs=[pl.BlockSpec((POOL_BUF, b, POOL_GC), lambda g: (0, 0, g)),
                  pl.BlockSpec((b, POOL_GC), lambda g: (0, g)),
                  pl.BlockSpec((1, POOL_GC, POOL_GC), lambda g: (g, 0, 0)),
                  pl.BlockSpec((1, 1, POOL_GC), lambda g: (g, 0, 0))],
        out_specs=pl.BlockSpec((b, POOL_GC), lambda g: (0, g)),
        out_shape=jax.ShapeDtypeStruct((b, POOL_WIDTH), BF16),
        compiler_params=_params("parallel"),
        name="pool_sample",
    )(prev_t, p, w_group.astype(BF16), scale.reshape(POOL_GROUPS, 1, POOL_GC))


def _attn_prompt_kernel(q_ref, kv_ref, o_ref):
    for h in range(MEM_HEADS):
        lo, hi = h * MEM_HEAD_DIM, (h + 1) * MEM_HEAD_DIM
        qh = q_ref[:, lo:hi].astype(BF16)
        kh = kv_ref[:, lo:hi].astype(BF16)
        vh = kv_ref[:, MEM_WIDTH + lo:MEM_WIDTH + hi].astype(BF16)
        s = lax.dot_general(qh, kh, _NT, preferred_element_type=F32) * MEM_SCALE
        e = jnp.exp(s - jnp.max(s, axis=-1, keepdims=True))
        l = jnp.sum(e, axis=-1, keepdims=True)
        o = jnp.dot(e.astype(BF16), vh, preferred_element_type=F32) / l
        o_ref[:, lo:hi] = o.astype(o_ref.dtype)


def attn_prompt(u, q_col, kv, *, tq=512):
    assert q_col % MEM_WIDTH == 0
    nt = SEQ // tq
    return pl.pallas_call(
        _attn_prompt_kernel,
        grid=(BATCH, nt),
        in_specs=[pl.BlockSpec((tq, MEM_WIDTH), lambda b, t: (b * nt + t, q_col // MEM_WIDTH)),
                  pl.BlockSpec((MEM_TOKENS, 2 * MEM_WIDTH), lambda b, t: (b, 0))],
        out_specs=pl.BlockSpec((tq, MEM_WIDTH), lambda b, t: (b * nt + t, 0)),
        out_shape=jax.ShapeDtypeStruct((N_PROMPT, MEM_WIDTH), BF16),
        compiler_params=_params("parallel", "parallel"),
        name="attn_prompt",
    )(u, kv)


def _attn_sample_kernel(q_ref, k_ref, v_ref, o_ref, acc_ref):
    i = pl.program_id(0)
    bb = k_ref.shape[0]
    for n in range(bb):
        tok = pl.ds(i * bb + n, 1)
        for h in range(MEM_HEADS):
            lo, hi = h * MEM_HEAD_DIM, (h + 1) * MEM_HEAD_DIM
            s = jnp.sum(k_ref[n, :, h, :] * q_ref[tok, lo:hi], axis=-1, keepdims=True) * MEM_SCALE
            e = jnp.exp(s - jnp.max(s, axis=0, keepdims=True))
            p = e / jnp.sum(e, axis=0, keepdims=True)
            acc_ref[tok, lo:hi] = jnp.sum(p * v_ref[n, :, h, :], axis=0, keepdims=True)

    @pl.when(i == pl.num_programs(0) - 1)
    def _():
        o_ref[...] = acc_ref[...].astype(o_ref.dtype)


def attn_sample(q, cache_k, cache_v, layer, *, bb=4):
    b = q.shape[0]
    cache_spec = pl.BlockSpec((None, bb, MEM_TOKENS, MEM_HEADS, MEM_HEAD_DIM),
                              lambda i: (layer, i, 0, 0, 0))
    return pl.pallas_call(
        _attn_sample_kernel,
        grid=(b // bb,),
        in_specs=[pl.BlockSpec((b, MEM_WIDTH), lambda i: (0, 0)), cache_spec, cache_spec],
        out_specs=pl.BlockSpec((b, MEM_WIDTH), lambda i: (0, 0)),
        out_shape=jax.ShapeDtypeStruct((b, MEM_WIDTH), BF16),
        scratch_shapes=[pltpu.VMEM((b, MEM_WIDTH), F32)],
        compiler_params=_params("arbitrary"),
        name="attn_sample",
    )(q, cache_k, cache_v)


def _log_decay(z, wg_ref, bg_ref):
    g = jnp.dot(z.astype(BF16), wg_ref[...], preferred_element_type=F32) + bg_ref[...]
    return jax.nn.log_sigmoid(g) / GATE_TAU


def _gla_out(o, og, gn):
    on = o * lax.rsqrt(jnp.mean(o * o, axis=-1, keepdims=True) + EPS) * gn
    return on * (og * jax.nn.sigmoid(og))


def _gla_prompt_kernel(q_ref, k_ref, v_ref, og_ref, z_ref, wg_ref, bg_ref, gn_ref,
                       mix_ref, st_ref, s_ref):
    n = pl.program_id(1)
    c = q_ref.shape[0]

    @pl.when(n == 0)
    def _():
        s_ref[...] = jnp.zeros_like(s_ref)

    la = _log_decay(z_ref[...], wg_ref, bg_ref)
    row = lax.broadcasted_iota(jnp.int32, (c, 1), 0)
    b = la
    k = 1
    while k < c:
        b = b + jnp.where(row >= k, pltpu.roll(b, k, 0), 0.0)
        k *= 2
    b_last = b[c - 1:c, :]
    q_in = q_ref[...] * jnp.exp(b) * GLA_SCALE
    k_in = k_ref[...] * jnp.exp(-b)
    k_out = k_ref[...] * jnp.exp(b_last - b)
    decay = jnp.exp(b_last)
    causal = row >= lax.broadcasted_iota(jnp.int32, (1, c), 1)
    gn = gn_ref[...]
    for h in range(GLA_HEADS):
        klo, khi = h * GLA_DK, (h + 1) * GLA_DK
        vlo, vhi = h * GLA_DV, (h + 1) * GLA_DV
        qh = q_in[:, klo:khi].astype(BF16)
        vh = v_ref[:, vlo:vhi].astype(BF16)
        a = lax.dot_general(qh, k_in[:, klo:khi].astype(BF16), _NT, preferred_element_type=F32)
        a = jnp.where(causal, a, 0.0)
        st = s_ref[h]
        o = (jnp.dot(a.astype(BF16), vh, preferred_element_type=F32)
             + lax.dot_general(qh, st.astype(BF16), _NT, preferred_element_type=F32))
        s_ref[h] = st * decay[:, klo:khi] + lax.dot_general(
            vh, k_out[:, klo:khi].astype(BF16), _TN, preferred_element_type=F32)
        mix_ref[:, vlo:vhi] = _gla_out(o, og_ref[:, vlo:vhi], gn).astype(mix_ref.dtype)

    @pl.when(n == pl.num_programs(1) - 1)
    def _():
        st_ref[0] = s_ref[...]


def gla_prompt(u, wg_pad, b_gate, norm_g):
    nc = SEQ // GLA_CHUNK
    c = GLA_CHUNK

    def col(width, off):
        assert off % width == 0
        return pl.BlockSpec((c, width), lambda b, n: (b * nc + n, off // width))

    return pl.pallas_call(
        _gla_prompt_kernel,
        grid=(BATCH, nc),
        in_specs=[col(GLA_KEY, GLA_COL_Q), col(GLA_KEY, GLA_COL_K), col(GLA_VAL, GLA_COL_V),
                  col(GLA_VAL, GLA_COL_OG), col(LANES, GLA_COL_Z),
                  pl.BlockSpec((LANES, GLA_KEY), lambda b, n: (0, 0)),
                  pl.BlockSpec((1, GLA_KEY), lambda b, n: (0, 0)),
                  pl.BlockSpec((1, GLA_DV), lambda b, n: (0, 0))],
        out_specs=[pl.BlockSpec((c, GLA_VAL), lambda b, n: (b * nc + n, 0)),
                   pl.BlockSpec((1, GLA_HEADS, GLA_DV, GLA_DK), lambda b, n: (b, 0, 0, 0))],
        out_shape=[jax.ShapeDtypeStruct((N_PROMPT, GLA_VAL), BF16),
                   jax.ShapeDtypeStruct((BATCH, GLA_HEADS, GLA_DV, GLA_DK), F32)],
        scratch_shapes=[pltpu.VMEM((GLA_HEADS, GLA_DV, GLA_DK), F32)],
        compiler_params=_params("parallel", "arbitrary"),
        name="gla_prompt",
    )(u, u, u, u, u, wg_pad, b_gate.reshape(1, GLA_KEY), norm_g.reshape(1, GLA_DV))


def _gla_sample_kernel(us_ref, s0_ref, wg_ref, bg_ref, gn_ref, mix_ref, s1_ref, la_ref):
    i = pl.program_id(0)

    @pl.when(i == 0)
    def _():
        la_ref[...] = _log_decay(us_ref[:, GLA_COL_Z:GLA_COL_Z + LANES], wg_ref, bg_ref)

    tok = pl.ds(i, 1)
    gn = gn_ref[...]
    for h in range(GLA_HEADS):
        klo, khi = h * GLA_DK, (h + 1) * GLA_DK
        vlo, vhi = h * GLA_DV, (h + 1) * GLA_DV
        la = la_ref[tok, klo:khi]
        q = us_ref[tok, GLA_COL_Q + klo:GLA_COL_Q + khi]
        k = us_ref[tok, GLA_COL_K + klo:GLA_COL_K + khi]
        v = us_ref[tok, GLA_COL_V + vlo:GLA_COL_V + vhi]
        og = us_ref[tok, GLA_COL_OG + vlo:GLA_COL_OG + vhi]
        q_in = q * jnp.exp(la) * GLA_SCALE
        k_in = k * jnp.exp(-la)
        k_out = k * jnp.exp(la - la)
        decay = jnp.exp(la)
        a = jnp.sum(q_in * k_in, axis=-1, keepdims=True)
        rows = jnp.concatenate(
            [q_in, k_out, decay, jnp.zeros((SUBLANES - 3, GLA_DK), F32)], axis=0)
        cols = jnp.transpose(rows)
        s0 = s0_ref[0, h]
        o = a * v + jnp.sum(cols[:, 0:1] * s0, axis=0, keepdims=True)
        s1_ref[0, h] = s0 * cols[:, 2:3] + cols[:, 1:2] * v
        mix_ref[tok, vlo:vhi] = _gla_out(o, og, gn).astype(mix_ref.dtype)


def gla_sample(us, s0, wg_pad, b_gate, norm_g):
    b = us.shape[0]
    return pl.pallas_call(
        _gla_sample_kernel,
        grid=(b,),
        in_specs=[pl.BlockSpec((b, GLA_IN_COLS), lambda i: (0, 0)),
                  pl.BlockSpec((1, GLA_HEADS, GLA_DK, GLA_DV), lambda i: (i, 0, 0, 0)),
                  pl.BlockSpec((LANES, GLA_KEY), lambda i: (0, 0)),
                  pl.BlockSpec((1, GLA_KEY), lambda i: (0, 0)),
                  pl.BlockSpec((1, GLA_DV), lambda i: (0, 0))],
        out_specs=[pl.BlockSpec((b, GLA_VAL), lambda i: (0, 0)),
                   pl.BlockSpec((1, GLA_HEADS, GLA_DK, GLA_DV), lambda i: (i, 0, 0, 0))],
        out_shape=[jax.ShapeDtypeStruct((b, GLA_VAL), F32),
                   jax.ShapeDtypeStruct(s0.shape, F32)],
        scratch_shapes=[pltpu.VMEM((b, GLA_KEY), F32)],
        compiler_params=_params("arbitrary"),
        name="gla_sample",
    )(us, s0, wg_pad, b_gate.reshape(1, GLA_KEY), norm_g.reshape(1, GLA_DV))


_PEER_CAND = tuple((a, b) for a in range(PEER_TOPK) for b in range(PEER_TOPK)
                   if (a + 1) * (b + 1) <= PEER_TOPK)
_NEG_INF = float("-inf")
_UNRANKED = float(N_KEYS)


_REDUCE_WIDTH = 8


def _reduce(op, vals):
    accs = list(vals[:_REDUCE_WIDTH])
    for n, v in enumerate(vals[_REDUCE_WIDTH:]):
        accs[n % _REDUCE_WIDTH] = op(accs[n % _REDUCE_WIDTH], v)
    while len(accs) > 1:
        accs = [op(accs[n], accs[n + 1]) if n + 1 < len(accs) else accs[n]
                for n in range(0, len(accs), 2)]
    return accs[0]


def _peer_select_kernel(q_ref, keys_ref, e0_ref, li_ref, e1_ref, rb_ref,
                        s_ref, cur_ref, rank_ref, sv_ref, ix_ref, cand_ref, cnt_ref, *, rt):
    n_rows = q_ref.shape[0] // LANES
    tile = (n_rows, LANES)

    def key_rows(k):
        return slice(k * n_rows, (k + 1) * n_rows)

    for p in range(2):
        for r in range(n_rows):
            qs = q_ref[r * LANES:(r + 1) * LANES, p * PEER_HALF:(p + 1) * PEER_HALF].astype(BF16)
            s_ref[p, pl.ds(r, N_KEYS, stride=n_rows), :] = lax.dot_general(
                keys_ref[p], qs, _NT, preferred_element_type=F32)

    for p in range(2):
        cur_ref[...] = s_ref[p]
        if p == 1:
            rank_ref[...] = jnp.full(rank_ref.shape, _UNRANKED, F32)

        def extract(r, m, p=p):
            idx = _reduce(jnp.minimum, [jnp.where(cur_ref[key_rows(k)] == m, float(k), _UNRANKED)
                                        for k in range(N_KEYS)])
            sv_ref[p, pl.ds(r, 1)] = m[None]
            if p == 0:
                ix_ref[pl.ds(r, 1)] = idx[None]
            rf = lax.convert_element_type(r, F32)
            rest = []
            for k in range(N_KEYS):
                hit = idx == float(k)
                c = jnp.where(hit, _NEG_INF, cur_ref[key_rows(k)])
                cur_ref[key_rows(k)] = c
                if p == 1:
                    rank_ref[key_rows(k)] = jnp.where(hit, rf, rank_ref[key_rows(k)])
                rest.append(c)
            return _reduce(jnp.maximum, rest)

        lax.fori_loop(0, PEER_TOPK, extract,
                      _reduce(jnp.maximum, [s_ref[p, key_rows(k)] for k in range(N_KEYS)]))

    for ci, (a, b) in enumerate(_PEER_CAND):
        cand_ref[ci] = sv_ref[0, a] + sv_ref[1, b]
    cnt_ref[...] = jnp.zeros_like(cnt_ref)
    c_max = sv_ref[0, 0] + sv_ref[1, 0]
    no_flat = float(PEER_TOPK * PEER_TOPK)

    def pick(_, z):
        m = _reduce(jnp.maximum, [cand_ref[ci] for ci in range(len(_PEER_CAND))])
        flat = _reduce(jnp.minimum, [
            jnp.where(cand_ref[ci] == m, float(a * PEER_TOPK + b), no_flat)
            for ci, (a, b) in enumerate(_PEER_CAND)])
        for ci, (a, b) in enumerate(_PEER_CAND):
            cand_ref[ci] = jnp.where(flat == float(a * PEER_TOPK + b), _NEG_INF, cand_ref[ci])
        first = jnp.floor(flat * (1.0 / PEER_TOPK))
        for a in range(PEER_TOPK):
            cnt_ref[a] = cnt_ref[a] + jnp.where(first == float(a), 1.0, 0.0)
        return z + jnp.exp(m - c_max)

    inv_z = 1.0 / lax.fori_loop(0, PEER_TOPK, pick, jnp.zeros(tile, F32))

    max0 = sv_ref[0, 0]
    max1 = sv_ref[1, 0]

    for k in range(N_KEYS):
        li = jnp.zeros(tile, F32)
        for a in range(PEER_TOPK):
            li = jnp.where(ix_ref[a] == float(k), cnt_ref[a], li)
        e0 = jnp.exp(s_ref[0, key_rows(k)] - max0) * inv_z
        e1 = jnp.exp(s_ref[1, key_rows(k)] - max1)
        rb = rank_ref[key_rows(k)]
        for b in range(n_rows // rt):
            src = slice(b * rt, (b + 1) * rt)
            dst = slice(k * rt, (k + 1) * rt)
            e0_ref[0, b, dst, :] = e0[src]
            li_ref[0, b, dst, :] = li[src]
            e1_ref[0, b, dst, :] = e1[src]
            rb_ref[0, b, dst, :] = rb[src]


def peer_select(q, keys_bf16, *, row0, rows, rblk, rt):
    assert rows % rblk == 0 and rblk % rt == 0 and row0 % rblk == 0
    tile = (rblk, LANES)
    nb = rblk // rt
    out = jax.ShapeDtypeStruct((PEER_HEADS, rows // rt, N_KEYS * rt, LANES), F32)
    out_spec = pl.BlockSpec((1, nb, N_KEYS * rt, LANES), lambda h, r: (h, r, 0, 0))
    return pl.pallas_call(
        functools.partial(_peer_select_kernel, rt=rt),
        grid=(PEER_HEADS, rows // rblk),
        in_specs=[pl.BlockSpec((rblk * LANES, 2 * PEER_HALF), lambda h, r: (r + row0 // rblk, h)),
                  pl.BlockSpec((2, N_KEYS, PEER_HALF), lambda h, r: (h, 0, 0))],
        out_specs=[out_spec] * 4,
        out_shape=[out] * 4,
        scratch_shapes=[pltpu.VMEM((2, N_KEYS * rblk, LANES), F32),
                        pltpu.VMEM((N_KEYS * rblk, LANES), F32),
                        pltpu.VMEM((N_KEYS * rblk, LANES), F32),
                        pltpu.VMEM((2, PEER_TOPK) + tile, F32),
                        pltpu.VMEM((PEER_TOPK,) + tile, F32),
                        pltpu.VMEM((len(_PEER_CAND),) + tile, F32),
                        pltpu.VMEM((PEER_TOPK,) + tile, F32)],
        compiler_params=_params("parallel", "parallel"),
        name="peer_select",
    )(q, keys_bf16)


_DENSE_CHUNK_KEYS = 32
_DENSE_GROUP_KEYS1 = 4
_DENSE_ROWS_PER_DOT = 512


def _gelu(x):
    return 0.5 * x * (1.0 + lax.erf(x * (2.0 ** -0.5)))


def _peer_dense_stages(ht_ref, u_ref, vt_ref, e0_ref, li_ref, e1_ref, rb_ref, o_ref,
                       a_wr, a_rd, w_wr, w_rd, *, ib):
    tn = ht_ref.shape[1]

    for r0 in range(0, ib * N_KEYS, _DENSE_ROWS_PER_DOT):
        rs = slice(r0, r0 + _DENSE_ROWS_PER_DOT)
        a_wr[rs, :] = jnp.dot(u_ref[rs, :], ht_ref[...], preferred_element_type=F32)

    jc = _DENSE_CHUNK_KEYS
    rt = tn // LANES
    for i0 in range(0, ib, _DENSE_GROUP_KEYS1):
        iis = range(i0, i0 + _DENSE_GROUP_KEYS1)
        for j0 in range(0, N_KEYS, jc):
            for c in range(rt):
                lanes = slice(c * LANES, (c + 1) * LANES)
                js = pl.ds(j0 * rt + c, jc, stride=rt)
                g = [None] * len(iis)
                for h in range(PEER_HEADS):
                    rb = rb_ref[h, 0, js, :]
                    e1 = e1_ref[h, 0, js, :]
                    for n, ii in enumerate(iis):
                        first = slice(ii * rt + c, ii * rt + c + 1)
                        e0 = e0_ref[h, 0, first, :]
                        li = li_ref[h, 0, first, :]
                        t = jnp.where(rb < li, e0 * e1, 0.0)
                        g[n] = t if g[n] is None else g[n] + t
                for n, ii in enumerate(iis):
                    rows = slice(ii * N_KEYS + j0, ii * N_KEYS + j0 + jc)
                    w_wr[rows, lanes] = (_gelu(a_rd[rows, lanes]) * g[n]).astype(w_wr.dtype)

    for r0 in range(0, vt_ref.shape[0], 2 * _DENSE_ROWS_PER_DOT):
        rs = slice(r0, r0 + 2 * _DENSE_ROWS_PER_DOT)
        o_ref[rs, :] += jnp.dot(vt_ref[rs, :], w_rd[...], preferred_element_type=F32)


def _peer_dense_kernel(ht_ref, u_ref, vt_ref, e0_ref, li_ref, e1_ref, rb_ref, o_ref,
                       a0_ref, a1_ref, w0_ref, w1_ref, *, ib, n_eb):
    s = pl.program_id(0)

    @pl.when(s == 0)
    def _():
        for ref in (a0_ref, a1_ref, w0_ref, w1_ref):
            ref[...] = jnp.zeros_like(ref)

    @pl.when((s < 2) | ((s - 2) % n_eb == 0))
    def _():
        o_ref[...] = jnp.zeros_like(o_ref)

    io = (ht_ref, u_ref, vt_ref, e0_ref, li_ref, e1_ref, rb_ref, o_ref)

    @pl.when(s % 2 == 0)
    def _():
        _peer_dense_stages(*io, a0_ref, a1_ref, w1_ref, w0_ref, ib=ib)

    @pl.when(s % 2 == 1)
    def _():
        _peer_dense_stages(*io, a1_ref, a0_ref, w0_ref, w1_ref, ib=ib)


def peer_dense(h_t, u_bf16, vt_bf16, e0, li, e1, rb, *, tok0, tn, ib=8):
    d = h_t.shape[0]
    rt = tn // LANES
    n_nb = e0.shape[1]
    ntok = n_nb * tn
    ebs = ib * N_KEYS
    assert e1.shape == (PEER_HEADS, n_nb, N_KEYS * rt, LANES) and tok0 % tn == 0
    assert N_KEYS % ib == 0 and ib % _DENSE_GROUP_KEYS1 == 0
    t0 = tok0 // tn
    n_eb = N_KEYS // ib
    total = n_nb * n_eb

    def stage(lag):
        return lambda s: jnp.clip(s - lag, 0, total - 1)

    fa, fb, fc = stage(0), stage(1), stage(2)
    return pl.pallas_call(
        functools.partial(_peer_dense_kernel, ib=ib, n_eb=n_eb),
        grid=(total + 2,),
        in_specs=[pl.BlockSpec((d, tn), lambda s: (0, fa(s) // n_eb + t0)),
                  pl.BlockSpec((ebs, d), lambda s: (fa(s) % n_eb, 0)),
                  pl.BlockSpec((d, ebs), lambda s: (0, fc(s) % n_eb)),
                  pl.BlockSpec((PEER_HEADS, 1, ib * rt, LANES),
                               lambda s: (0, fb(s) // n_eb, fb(s) % n_eb, 0)),
                  pl.BlockSpec((PEER_HEADS, 1, ib * rt, LANES),
                               lambda s: (0, fb(s) // n_eb, fb(s) % n_eb, 0)),
                  pl.BlockSpec((PEER_HEADS, 1, N_KEYS * rt, LANES),
                               lambda s: (0, fb(s) // n_eb, 0, 0)),
                  pl.BlockSpec((PEER_HEADS, 1, N_KEYS * rt, LANES),
                               lambda s: (0, fb(s) // n_eb, 0, 0))],
        out_specs=pl.BlockSpec((d, tn), lambda s: (0, fc(s) // n_eb)),
        out_shape=jax.ShapeDtypeStruct((d, ntok), F32),
        scratch_shapes=[pltpu.VMEM((ebs, tn), F32), pltpu.VMEM((ebs, tn), F32),
                        pltpu.VMEM((ebs, tn), BF16), pltpu.VMEM((ebs, tn), BF16)],
        compiler_params=_params("arbitrary"),
        name="peer_dense",
    )(h_t, u_bf16, vt_bf16, e0, li, e1, rb)


def _table_kernel(x_ref, o_ref, *, transposed):
    x = x_ref[...]
    o_ref[...] = (jnp.transpose(x) if transposed else x).astype(o_ref.dtype)


def expert_table_bf16(tabs, layer, *, transposed, rows=512):
    _, e, d = tabs.shape
    assert e % rows == 0
    if transposed:
        out_shape, out_spec = (d, e), pl.BlockSpec((d, rows), lambda i: (0, i))
    else:
        out_shape, out_spec = (e, d), pl.BlockSpec((rows, d), lambda i: (i, 0))
    return pl.pallas_call(
        functools.partial(_table_kernel, transposed=transposed),
        grid=(e // rows,),
        in_specs=[pl.BlockSpec((None, rows, d), lambda i: (layer, i, 0))],
        out_specs=out_spec,
        out_shape=jax.ShapeDtypeStruct(out_shape, BF16),
        compiler_params=_params("parallel"),
        name="expert_table",
    )(tabs)


def _add_transposed_kernel(x_ref, yt_ref, o_ref):
    o_ref[...] = x_ref[...] + jnp.transpose(yt_ref[...])


def add_transposed(x, y_t, *, tm):
    n, d = x.shape
    return pl.pallas_call(
        _add_transposed_kernel,
        grid=(n // tm,),
        in_specs=[pl.BlockSpec((tm, d), lambda i: (i, 0)), pl.BlockSpec((d, tm), lambda i: (0, i))],
        out_specs=pl.BlockSpec((tm, d), lambda i: (i, 0)),
        out_shape=jax.ShapeDtypeStruct((n, d), F32),
        compiler_params=_params("parallel"),
        name="add_transposed",
    )(x, y_t)


def peer_layer(x, norm_g, w_q, keys, u_tabs, v_tabs, layer):
    h, h_t = rmsnorm(x, norm_g, BF16, tm=640, transposed=True)
    q = matmul([h], [w_q.astype(BF16)], tm=1664, tn=512)
    keys_bf16 = keys.reshape(2 * PEER_HEADS, N_KEYS, PEER_HALF).astype(BF16)
    tn_p = 512
    sel_p = peer_select(q, keys_bf16, row0=0, rows=N_PROMPT // LANES, rblk=32, rt=tn_p // LANES)
    sel_s = peer_select(q, keys_bf16, row0=N_PROMPT // LANES, rows=DEC_BATCH // LANES, rblk=1, rt=1)
    u_bf16 = expert_table_bf16(u_tabs, layer, transposed=False)
    vt_bf16 = expert_table_bf16(v_tabs, layer, transposed=True)
    out_p = peer_dense(h_t, u_bf16, vt_bf16, *sel_p, tok0=0, tn=tn_p)
    out_s = peer_dense(h_t, u_bf16, vt_bf16, *sel_s, tok0=N_PROMPT, tn=DEC_BATCH)
    return add_transposed(x, jnp.concatenate([out_p, out_s], axis=1), tm=640)


def kernel(x_prompt, x_sample, cache_mem_k, cache_mem_v, state_pool, state_gla, mem_prompt,
           norm_mix, norm_ffn, norm_final, mem_norm, w_mem_kv,
           pool_w_in, pool_w_group, pool_scale, pool_w_out,
           gla_w_in, gla_w_gate, gla_b_gate, gla_norm, gla_w_out,
           peer_w_q, peer_keys, peer_u, peer_v):
    d = D_MODEL
    x = jnp.concatenate([x_prompt.reshape(N_PROMPT, d), x_sample.reshape(DEC_BATCH, d)], axis=0)

    mem = mem_prompt.reshape(BATCH * MEM_TOKENS, d)
    kvs = []
    for i in range(DEPTH):
        mn = rmsnorm(mem, mem_norm[i], BF16, tm=512)
        kvs.append(matmul([mn], [w_mem_kv[i].astype(BF16)], tm=512, tn=512))

    def split_kv(kv, lo):
        return kv[:, lo:lo + MEM_WIDTH].reshape(BATCH, MEM_TOKENS, MEM_HEADS, MEM_HEAD_DIM)

    mem_k_prompt = jnp.stack([split_kv(kv, 0) for kv in kvs])
    mem_v_prompt = jnp.stack([split_kv(kv, MEM_WIDTH) for kv in kvs])

    h = rmsnorm(x, norm_mix[0], BF16, tm=640)
    u = matmul([h], [pool_w_in[0].astype(BF16)], tm=1664, tn=512)
    us = u[N_PROMPT:]
    p_s = us[:, :POOL_WIDTH]
    mix = jnp.concatenate([
        pool_prompt(u, pool_w_group[0], pool_scale[0]),
        pool_sample(jnp.transpose(state_pool[0], (1, 0, 2)), p_s, pool_w_group[0], pool_scale[0]),
    ], axis=0)
    att = jnp.concatenate([
        attn_prompt(u, POOL_WIDTH, kvs[0]),
        attn_sample(us[:, POOL_WIDTH:], cache_mem_k, cache_mem_v, 0),
    ], axis=0)
    w_out = pool_w_out[0].astype(BF16)
    x = matmul([mix, att], [w_out[:POOL_WIDTH], w_out[POOL_WIDTH:]], x, tm=1664, tn=512)
    pool_prompt_state = u[:N_PROMPT, :POOL_WIDTH].reshape(BATCH, SEQ, POOL_WIDTH)[:, SEQ - POOL_BUF:][None]
    pool_sample_state = jnp.concatenate([state_pool[0][:, 1:], p_s[:, None, :]], axis=1)[None]
    x = peer_layer(x, norm_ffn[0], peer_w_q[0], peer_keys[0], peer_u, peer_v, 0)

    w_in = gla_w_in[0]
    o3 = 2 * GLA_KEY + GLA_VAL
    o4 = o3 + GATE_RANK
    o5 = o4 + GLA_VAL
    w_in = jnp.concatenate([
        w_in[:, :o3], w_in[:, o4:o5], w_in[:, o5:], w_in[:, o3:o4],
        jnp.zeros((d, GLA_Z_PAD - GATE_RANK), w_in.dtype)], axis=1).astype(BF16)
    wg_pad = jnp.concatenate(
        [gla_w_gate[0], jnp.zeros((LANES - GATE_RANK, GLA_KEY), gla_w_gate.dtype)], axis=0).astype(BF16)
    h = rmsnorm(x, norm_mix[1], BF16, tm=640)
    u = matmul([h], [w_in], tm=1664, tn=512)
    us = u[N_PROMPT:]
    mix_p, st_t = gla_prompt(u, wg_pad, gla_b_gate[0], gla_norm[0])
    mix_s, gla_s = gla_sample(us, state_gla[0], wg_pad, gla_b_gate[0], gla_norm[0])
    mix = jnp.concatenate([mix_p, mix_s.astype(BF16)], axis=0)
    att = jnp.concatenate([
        attn_prompt(u, GLA_COL_QM, kvs[1]),
        attn_sample(us[:, GLA_COL_QM:GLA_COL_QM + MEM_WIDTH], cache_mem_k, cache_mem_v, 1),
    ], axis=0)
    w_out = gla_w_out[0].astype(BF16)
    x = matmul([mix, att], [w_out[:GLA_VAL], w_out[GLA_VAL:]], x, tm=1664, tn=512)
    x = peer_layer(x, norm_ffn[1], peer_w_q[1], peer_keys[1], peer_u, peer_v, 1)

    y_prompt = rmsnorm(x, norm_final, F32, tm=512, row0=0, rows=N_PROMPT)
    y_sample = rmsnorm(x, norm_final, F32, tm=DEC_BATCH, row0=N_PROMPT, rows=DEC_BATCH)
    return (y_prompt.reshape(BATCH, SEQ, d),
            y_sample.reshape(DEC_BATCH, 1, d),
            pool_prompt_state,
            jnp.swapaxes(st_t, -1, -2)[None],
            mem_k_prompt,
            mem_v_prompt,
            pool_sample_state,
            gla_s[None])
```

```python
import functools

import jax
import jax.numpy as jnp
from jax import lax
from jax.experimental import pallas as pl
from jax.experimental.pallas import tpu as pltpu

D_MODEL = 2048
BATCH = 4
SEQ = 2048
DEPTH = 2
DEC_BATCH = 128
PAST_LEN = 16384
EPS = 1e-6

N_PROMPT = BATCH * SEQ
N_TOK = N_PROMPT + DEC_BATCH

POOL_WIDTH = D_MODEL // 2
POOL_GROUPS = 4
POOL_GC = POOL_WIDTH // POOL_GROUPS
POOL_WINDOWS = (2, 4, 8, 16)
POOL_BUF = max(POOL_WINDOWS) - 1

MEM_TOKENS = 256
MEM_HEADS = 4
MEM_HEAD_DIM = D_MODEL // 8
MEM_WIDTH = MEM_HEADS * MEM_HEAD_DIM
MEM_SCALE = MEM_HEAD_DIM ** -0.5

GLA_HEADS = 4
GLA_KEY = D_MODEL // 2
GLA_VAL = D_MODEL
GLA_DK = GLA_KEY // GLA_HEADS
GLA_DV = GLA_VAL // GLA_HEADS
GATE_RANK = 16
GATE_TAU = 16.0
GLA_CHUNK = 64
GLA_SCALE = GLA_DK ** -0.5

N_KEYS = 128
N_EXPERTS = N_KEYS * N_KEYS
PEER_HEADS = 8
PEER_HALF = 128
PEER_TOPK = 16

LANES = 128
SUBLANES = 8
VMEM_LIMIT = 56 * 1024 * 1024

GLA_COL_Q = 0
GLA_COL_K = GLA_KEY
GLA_COL_V = 2 * GLA_KEY
GLA_COL_OG = 2 * GLA_KEY + GLA_VAL
GLA_COL_QM = 2 * GLA_KEY + 2 * GLA_VAL
GLA_COL_Z = GLA_COL_QM + MEM_WIDTH
GLA_Z_PAD = 512
GLA_IN_COLS = GLA_COL_Z + GLA_Z_PAD

BF16 = jnp.bfloat16
F32 = jnp.float32

_NT = (((1,), (1,)), ((), ()))
_TN = (((0,), (0,)), ((), ()))


def _params(*sem):
    return pltpu.CompilerParams(dimension_semantics=sem, vmem_limit_bytes=VMEM_LIMIT)


def _rmsnorm_kernel(x_ref, g_ref, *o_refs):
    x = x_ref[...]
    y = x * lax.rsqrt(jnp.mean(x * x, axis=-1, keepdims=True) + EPS) * g_ref[...]
    o_refs[0][...] = y.astype(o_refs[0].dtype)
    if len(o_refs) > 1:
        o_refs[1][...] = jnp.transpose(y).astype(o_refs[1].dtype)


def rmsnorm(x, g, out_dtype, *, tm, transposed=False):
    rows, d = x.shape
    assert rows % tm == 0
    out_shape = [jax.ShapeDtypeStruct((rows, d), out_dtype)]
    out_specs = [pl.BlockSpec((tm, d), lambda i: (i, 0))]
    if transposed:
        out_shape.append(jax.ShapeDtypeStruct((d, rows), out_dtype))
        out_specs.append(pl.BlockSpec((d, tm), lambda i: (0, i)))
    res = pl.pallas_call(
        _rmsnorm_kernel,
        grid=(rows // tm,),
        in_specs=[pl.BlockSpec((tm, d), lambda i: (i, 0)),
                  pl.BlockSpec((1, d), lambda i: (0, 0))],
        out_specs=out_specs,
        out_shape=out_shape,
        compiler_params=_params("parallel"),
        name="rmsnorm",
    )(x, g.reshape(1, d))
    return res if transposed else res[0]


def _matmul_kernel(*refs, n_pairs, has_res):
    o_ref = refs[-1]
    acc = None
    for a_ref, w_ref in zip(refs[:n_pairs], refs[n_pairs:2 * n_pairs]):
        p = jnp.dot(a_ref[...], w_ref[...], preferred_element_type=F32)
        acc = p if acc is None else acc + p
    if has_res:
        acc = acc + refs[2 * n_pairs][...]
    o_ref[...] = acc.astype(o_ref.dtype)


def matmul(a_list, w_list, residual=None, *, tm, tn):
    n = a_list[0].shape[0]
    f = w_list[0].shape[1]
    assert n % tm == 0 and f % tn == 0
    in_specs = [pl.BlockSpec((tm, a.shape[1]), lambda i, j: (i, 0)) for a in a_list]
    in_specs += [pl.BlockSpec((w.shape[0], tn), lambda i, j: (0, j)) for w in w_list]
    args = list(a_list) + list(w_list)
    if residual is not None:
        in_specs.append(pl.BlockSpec((tm, tn), lambda i, j: (i, j)))
        args.append(residual)
    return pl.pallas_call(
        functools.partial(_matmul_kernel, n_pairs=len(a_list), has_res=residual is not None),
        grid=(n // tm, f // tn),
        in_specs=in_specs,
        out_specs=pl.BlockSpec((tm, tn), lambda i, j: (i, j)),
        out_shape=jax.ShapeDtypeStruct((n, f), F32),
        compiler_params=_params("parallel", "parallel"),
        name="matmul",
    )(*args)


def _pool_prompt_kernel(p_ref, wg_ref, sc_ref, o_ref):
    g = pl.program_id(1)
    x = p_ref[...]
    t_len = x.shape[0]
    row = lax.broadcasted_iota(jnp.int32, (t_len, 1), 0)

    def shifted(v, k):
        return jnp.where(row >= k, pltpu.roll(v, k, 0), 0.0)

    for gi, w in enumerate(POOL_WINDOWS):
        @pl.when(g == gi)
        def _(w=w):
            s = x
            k = 1
            while k < w:
                s = s + shifted(s, k)
                k *= 2
            cnt = jnp.minimum(w, row + 1).astype(F32)
            d = s / cnt - x
            out = jnp.dot(d.astype(BF16), wg_ref[0], preferred_element_type=F32)
            o_ref[...] = (out * sc_ref[0]).astype(o_ref.dtype)


def pool_prompt(u, w_group, scale):
    return pl.pallas_call(
        _pool_prompt_kernel,
        grid=(BATCH, POOL_GROUPS),
        in_specs=[pl.BlockSpec((SEQ, POOL_GC), lambda b, g: (b, g)),
                  pl.BlockSpec((1, POOL_GC, POOL_GC), lambda b, g: (g, 0, 0)),
                  pl.BlockSpec((1, 1, POOL_GC), lambda b, g: (g, 0, 0))],
        out_specs=pl.BlockSpec((SEQ, POOL_GC), lambda b, g: (b, g)),
        out_shape=jax.ShapeDtypeStruct((N_PROMPT, POOL_WIDTH), BF16),
        compiler_params=_params("parallel", "parallel"),
        name="pool_prompt",
    )(u, w_group.astype(BF16), scale.reshape(POOL_GROUPS, 1, POOL_GC))


def _pool_sample_kernel(prev_ref, p_ref, wg_ref, sc_ref, o_ref):
    g = pl.program_id(0)
    x = p_ref[...]
    for gi, w in enumerate(POOL_WINDOWS):
        @pl.when(g == gi)
        def _(w=w):
            s = x
            for r in range(POOL_BUF - (w - 1), POOL_BUF):
                s = s + prev_ref[r]
            cnt = float(min(w, PAST_LEN + 1))
            d = s / cnt - x
            out = jnp.dot(d.astype(BF16), wg_ref[0], preferred_element_type=F32)
            o_ref[...] = (out * sc_ref[0]).astype(o_ref.dtype)


def pool_sample(prev_t, p, w_group, scale):
    b = p.shape[0]
    return pl.pallas_call(
        _pool_sample_kernel,
        grid=(POOL_GROUPS,),
        in_specs=[pl.BlockSpec((POOL_BUF, b, POOL_GC), lambda g: (0, 0, g)),
                  pl.BlockSpec((b, POOL_GC), lambda g: (0, g)),
                  pl.BlockSpec((1, POOL_GC, POOL_GC), lambda g: (g, 0, 0)),
                  pl.BlockSpec((1, 1, POOL_GC), lambda g: (g, 0, 0))],
        out_specs=pl.BlockSpec((b, POOL_GC), lambda g: (0, g)),
        out_shape=jax.ShapeDtypeStruct((b, POOL_WIDTH), BF16),
        compiler_params=_params("parallel"),
        name="pool_sample",
    )(prev_t, p, w_group.astype(BF16), scale.reshape(POOL_GROUPS, 1, POOL_GC))


def _attn_prompt_kernel(q_ref, kv_ref, o_ref):
    for h in range(MEM_HEADS):
        lo, hi = h * MEM_HEAD_DIM, (h + 1) * MEM_HEAD_DIM
        qh = q_ref[:, lo:hi].astype(BF16)
        kh = kv_ref[:, lo:hi].astype(BF16)
        vh = kv_ref[:, MEM_WIDTH + lo:MEM_WIDTH + hi].astype(BF16)
        s = lax.dot_general(qh, kh, _NT, preferred_element_type=F32) * MEM_SCALE
        e = jnp.exp(s - jnp.max(s, axis=-1, keepdims=True))
        l = jnp.sum(e, axis=-1, keepdims=True)
        o = jnp.dot(e.astype(BF16), vh, preferred_element_type=F32) / l
        o_ref[:, lo:hi] = o.astype(o_ref.dtype)


def attn_prompt(u, q_col, kv, *, tq=512):
    assert q_col % MEM_WIDTH == 0
    nt = SEQ // tq
    return pl.pallas_call(
        _attn_prompt_kernel,
        grid=(BATCH, nt),
        in_specs=[pl.BlockSpec((tq, MEM_WIDTH), lambda b, t: (b * nt + t, q_col // MEM_WIDTH)),
                  pl.BlockSpec((MEM_TOKENS, 2 * MEM_WIDTH), lambda b, t: (b, 0))],
        out_specs=pl.BlockSpec((tq, MEM_WIDTH), lambda b, t: (b * nt + t, 0)),
        out_shape=jax.ShapeDtypeStruct((N_PROMPT, MEM_WIDTH), BF16),
        compiler_params=_params("parallel", "parallel"),
        name="attn_prompt",
    )(u, kv)


def _attn_sample_kernel(q_ref, k_ref, v_ref, o_ref, acc_ref):
    i = pl.program_id(0)
    bb = k_ref.shape[0]
    for n in range(bb):
        tok = pl.ds(i * bb + n, 1)
        for h in range(MEM_HEADS):
            lo, hi = h * MEM_HEAD_DIM, (h + 1) * MEM_HEAD_DIM
            s = jnp.sum(k_ref[n, :, h, :] * q_ref[tok, lo:hi], axis=-1, keepdims=True) * MEM_SCALE
            e = jnp.exp(s - jnp.max(s, axis=0, keepdims=True))
            p = e / jnp.sum(e, axis=0, keepdims=True)
            acc_ref[tok, lo:hi] = jnp.sum(p * v_ref[n, :, h, :], axis=0, keepdims=True)

    @pl.when(i == pl.num_programs(0) - 1)
    def _():
        o_ref[...] = acc_ref[...].astype(o_ref.dtype)


def attn_sample(q, cache_k, cache_v, layer, *, bb=4):
    b = q.shape[0]
    cache_spec = pl.BlockSpec((None, bb, MEM_TOKENS, MEM_HEADS, MEM_HEAD_DIM),
                              lambda i: (layer, i, 0, 0, 0))
    return pl.pallas_call(
        _attn_sample_kernel,
        grid=(b // bb,),
        in_specs=[pl.BlockSpec((b, MEM_WIDTH), lambda i: (0, 0)), cache_spec, cache_spec],
        out_specs=pl.BlockSpec((b, MEM_WIDTH), lambda i: (0, 0)),
        out_shape=jax.ShapeDtypeStruct((b, MEM_WIDTH), BF16),
        scratch_shapes=[pltpu.VMEM((b, MEM_WIDTH), F32)],
        compiler_params=_params("arbitrary"),
        name="attn_sample",
    )(q, cache_k, cache_v)


def _log_decay(z, wg_ref, bg_ref):
    g = jnp.dot(z.astype(BF16), wg_ref[...], preferred_element_type=F32) + bg_ref[...]
    return jax.nn.log_sigmoid(g) / GATE_TAU


def _gla_out(o, og, gn):
    on = o * lax.rsqrt(jnp.mean(o * o, axis=-1, keepdims=True) + EPS) * gn
    return on * (og * jax.nn.sigmoid(og))


def _gla_prompt_kernel(q_ref, k_ref, v_ref, og_ref, z_ref, wg_ref, bg_ref, gn_ref,
                       mix_ref, st_ref, s_ref):
    n = pl.program_id(1)
    c = q_ref.shape[0]

    @pl.when(n == 0)
    def _():
        s_ref[...] = jnp.zeros_like(s_ref)

    la = _log_decay(z_ref[...], wg_ref, bg_ref)
    row = lax.broadcasted_iota(jnp.int32, (c, 1), 0)
    b = la
    k = 1
    while k < c:
        b = b + jnp.where(row >= k, pltpu.roll(b, k, 0), 0.0)
        k *= 2
    b_last = b[c - 1:c, :]
    q_in = q_ref[...] * jnp.exp(b) * GLA_SCALE
    k_in = k_ref[...] * jnp.exp(-b)
    k_out = k_ref[...] * jnp.exp(b_last - b)
    decay = jnp.exp(b_last)
    causal = row >= lax.broadcasted_iota(jnp.int32, (1, c), 1)
    gn = gn_ref[...]
    for h in range(GLA_HEADS):
        klo, khi = h * GLA_DK, (h + 1) * GLA_DK
        vlo, vhi = h * GLA_DV, (h + 1) * GLA_DV
        qh = q_in[:, klo:khi].astype(BF16)
        vh = v_ref[:, vlo:vhi].astype(BF16)
        a = lax.dot_general(qh, k_in[:, klo:khi].astype(BF16), _NT, preferred_element_type=F32)
        a = jnp.where(causal, a, 0.0)
        st = s_ref[h]
        o = (jnp.dot(a.astype(BF16), vh, preferred_element_type=F32)
             + lax.dot_general(qh, st.astype(BF16), _NT, preferred_element_type=F32))
        s_ref[h] = st * decay[:, klo:khi] + lax.dot_general(
            vh, k_out[:, klo:khi].astype(BF16), _TN, preferred_element_type=F32)
        mix_ref[:, vlo:vhi] = _gla_out(o, og_ref[:, vlo:vhi], gn).astype(mix_ref.dtype)

    @pl.when(n == pl.num_programs(1) - 1)
    def _():
        st_ref[0] = s_ref[...]


def gla_prompt(u, wg_pad, b_gate, norm_g):
    nc = SEQ // GLA_CHUNK
    c = GLA_CHUNK

    def col(width, off):
        assert off % width == 0
        return pl.BlockSpec((c, width), lambda b, n: (b * nc + n, off // width))

    return pl.pallas_call(
        _gla_prompt_kernel,
        grid=(BATCH, nc),
        in_specs=[col(GLA_KEY, GLA_COL_Q), col(GLA_KEY, GLA_COL_K), col(GLA_VAL, GLA_COL_V),
                  col(GLA_VAL, GLA_COL_OG), col(LANES, GLA_COL_Z),
                  pl.BlockSpec((LANES, GLA_KEY), lambda b, n: (0, 0)),
                  pl.BlockSpec((1, GLA_KEY), lambda b, n: (0, 0)),
                  pl.BlockSpec((1, GLA_DV), lambda b, n: (0, 0))],
        out_specs=[pl.BlockSpec((c, GLA_VAL), lambda b, n: (b * nc + n, 0)),
                   pl.BlockSpec((1, GLA_HEADS, GLA_DV, GLA_DK), lambda b, n: (b, 0, 0, 0))],
        out_shape=[jax.ShapeDtypeStruct((N_PROMPT, GLA_VAL), BF16),
                   jax.ShapeDtypeStruct((BATCH, GLA_HEADS, GLA_DV, GLA_DK), F32)],
        scratch_shapes=[pltpu.VMEM((GLA_HEADS, GLA_DV, GLA_DK), F32)],
        compiler_params=_params("parallel", "arbitrary"),
        name="gla_prompt",
    )(u, u, u, u, u, wg_pad, b_gate.reshape(1, GLA_KEY), norm_g.reshape(1, GLA_DV))


def _gla_sample_kernel(us_ref, s0_ref, wg_ref, bg_ref, gn_ref, mix_ref, s1_ref, la_ref):
    i = pl.program_id(0)

    @pl.when(i == 0)
    def _():
        la_ref[...] = _log_decay(us_ref[:, GLA_COL_Z:GLA_COL_Z + LANES], wg_ref, bg_ref)

    tok = pl.ds(i, 1)
    gn = gn_ref[...]
    for h in range(GLA_HEADS):
        klo, khi = h * GLA_DK, (h + 1) * GLA_DK
        vlo, vhi = h * GLA_DV, (h + 1) * GLA_DV
        la = la_ref[tok, klo:khi]
        q = us_ref[tok, GLA_COL_Q + klo:GLA_COL_Q + khi]
        k = us_ref[tok, GLA_COL_K + klo:GLA_COL_K + khi]
        v = us_ref[tok, GLA_COL_V + vlo:GLA_COL_V + vhi]
        og = us_ref[tok, GLA_COL_OG + vlo:GLA_COL_OG + vhi]
        q_in = q * jnp.exp(la) * GLA_SCALE
        k_in = k * jnp.exp(-la)
        k_out = k * jnp.exp(la - la)
        decay = jnp.exp(la)
        a = jnp.sum(q_in * k_in, axis=-1, keepdims=True)
        rows = jnp.concatenate(
            [q_in, k_out, decay, jnp.zeros((SUBLANES - 3, GLA_DK), F32)], axis=0)
        cols = jnp.transpose(rows)
        s0 = s0_ref[0, h]
        o = a * v + jnp.sum(cols[:, 0:1] * s0, axis=0, keepdims=True)
        s1_ref[0, h] = s0 * cols[:, 2:3] + cols[:, 1:2] * v
        mix_ref[tok, vlo:vhi] = _gla_out(o, og, gn).astype(mix_ref.dtype)


def gla_sample(us, s0, wg_pad, b_gate, norm_g):
    b = us.shape[0]
    return pl.pallas_call(
        _gla_sample_kernel,
        grid=(b,),
        in_specs=[pl.BlockSpec((b, GLA_IN_COLS), lambda i: (0, 0)),
                  pl.BlockSpec((1, GLA_HEADS, GLA_DK, GLA_DV), lambda i: (i, 0, 0, 0)),
                  pl.BlockSpec((LANES, GLA_KEY), lambda i: (0, 0)),
                  pl.BlockSpec((1, GLA_KEY), lambda i: (0, 0)),
                  pl.BlockSpec((1, GLA_DV), lambda i: (0, 0))],
        out_specs=[pl.BlockSpec((b, GLA_VAL), lambda i: (0, 0)),
                   pl.BlockSpec((1, GLA_HEADS, GLA_DK, GLA_DV), lambda i: (i, 0, 0, 0))],
        out_shape=[jax.ShapeDtypeStruct((b, GLA_VAL), F32),
                   jax.ShapeDtypeStruct(s0.shape, F32)],
        scratch_shapes=[pltpu.VMEM((b, GLA_KEY), F32)],
        compiler_params=_params("arbitrary"),
        name="gla_sample",
    )(us, s0, wg_pad, b_gate.reshape(1, GLA_KEY), norm_g.reshape(1, GLA_DV))


_PEER_CAND = tuple((a, b) for a in range(PEER_TOPK) for b in range(PEER_TOPK)
                   if (a + 1) * (b + 1) <= PEER_TOPK)
_NEG_INF = float("-inf")
_UNRANKED = float(N_KEYS)


_REDUCE_WIDTH = 8


def _reduce(op, vals):
    accs = list(vals[:_REDUCE_WIDTH])
    for n, v in enumerate(vals[_REDUCE_WIDTH:]):
        accs[n % _REDUCE_WIDTH] = op(accs[n % _REDUCE_WIDTH], v)
    while len(accs) > 1:
        accs = [op(accs[n], accs[n + 1]) if n + 1 < len(accs) else accs[n]
                for n in range(0, len(accs), 2)]
    return accs[0]


def _peer_select_kernel(q_ref, keys_ref, e0_ref, li_ref, e1_ref, rb_ref,
                        s_ref, cur_ref, rank_ref, sv_ref, ix_ref, cand_ref, cnt_ref, *, rt):
    n_rows = q_ref.shape[0] // LANES
    tile = (n_rows, LANES)

    def key_rows(k):
        return slice(k * n_rows, (k + 1) * n_rows)

    for p in range(2):
        for r in range(n_rows):
            qs = q_ref[r * LANES:(r + 1) * LANES, p * PEER_HALF:(p + 1) * PEER_HALF].astype(BF16)
            s_ref[p, pl.ds(r, N_KEYS, stride=n_rows), :] = lax.dot_general(
                keys_ref[p], qs, _NT, preferred_element_type=F32)

    for p in range(2):
        cur_ref[...] = s_ref[p]
        if p == 1:
            rank_ref[...] = jnp.full(rank_ref.shape, _UNRANKED, F32)

        def extract(r, m, p=p):
            idx = _reduce(jnp.minimum, [jnp.where(cur_ref[key_rows(k)] == m, float(k), _UNRANKED)
                                        for k in range(N_KEYS)])
            sv_ref[p, pl.ds(r, 1)] = m[None]
            if p == 0:
                ix_ref[pl.ds(r, 1)] = idx[None]
            rf = lax.convert_element_type(r, F32)
            rest = []
            for k in range(N_KEYS):
                hit = idx == float(k)
                c = jnp.where(hit, _NEG_INF, cur_ref[key_rows(k)])
                cur_ref[key_rows(k)] = c
                if p == 1:
                    rank_ref[key_rows(k)] = jnp.where(hit, rf, rank_ref[key_rows(k)])
                rest.append(c)
            return _reduce(jnp.maximum, rest)

        lax.fori_loop(0, PEER_TOPK, extract,
                      _reduce(jnp.maximum, [s_ref[p, key_rows(k)] for k in range(N_KEYS)]))

    for ci, (a, b) in enumerate(_PEER_CAND):
        cand_ref[ci] = sv_ref[0, a] + sv_ref[1, b]
    cnt_ref[...] = jnp.zeros_like(cnt_ref)
    c_max = sv_ref[0, 0] + sv_ref[1, 0]
    no_flat = float(PEER_TOPK * PEER_TOPK)

    def pick(_, z):
        m = _reduce(jnp.maximum, [cand_ref[ci] for ci in range(len(_PEER_CAND))])
        flat = _reduce(jnp.minimum, [
            jnp.where(cand_ref[ci] == m, float(a * PEER_TOPK + b), no_flat)
            for ci, (a, b) in enumerate(_PEER_CAND)])
        for ci, (a, b) in enumerate(_PEER_CAND):
            cand_ref[ci] = jnp.where(flat == float(a * PEER_TOPK + b), _NEG_INF, cand_ref[ci])
        first = jnp.floor(flat * (1.0 / PEER_TOPK))
        for a in range(PEER_TOPK):
            cnt_ref[a] = cnt_ref[a] + jnp.where(first == float(a), 1.0, 0.0)
        return z + jnp.exp(m - c_max)

    inv_z = 1.0 / lax.fori_loop(0, PEER_TOPK, pick, jnp.zeros(tile, F32))

    max0 = sv_ref[0, 0]
    max1 = sv_ref[1, 0]

    for k in range(N_KEYS):
        li = jnp.zeros(tile, F32)
        for a in range(PEER_TOPK):
            li = jnp.where(ix_ref[a] == float(k), cnt_ref[a], li)
        e0 = jnp.exp(s_ref[0, key_rows(k)] - max0) * inv_z
        e1 = jnp.exp(s_ref[1, key_rows(k)] - max1)
        rb = rank_ref[key_rows(k)]
        for b in range(n_rows // rt):
            src = slice(b * rt, (b + 1) * rt)
            dst = slice(k * rt, (k + 1) * rt)
            e0_ref[0, b, dst, :] = e0[src]
            li_ref[0, b, dst, :] = li[src]
            e1_ref[0, b, dst, :] = e1[src]
            rb_ref[0, b, dst, :] = rb[src]


def peer_select(q, keys_bf16, *, rows, rblk, rt):
    assert rows % rblk == 0 and rblk % rt == 0
    tile = (rblk, LANES)
    nb = rblk // rt
    out = jax.ShapeDtypeStruct((PEER_HEADS, rows // rt, N_KEYS * rt, LANES), F32)
    out_spec = pl.BlockSpec((1, nb, N_KEYS * rt, LANES), lambda h, r: (h, r, 0, 0))
    return pl.pallas_call(
        functools.partial(_peer_select_kernel, rt=rt),
        grid=(PEER_HEADS, rows // rblk),
        in_specs=[pl.BlockSpec((rblk * LANES, 2 * PEER_HALF), lambda h, r: (r, h)),
                  pl.BlockSpec((2, N_KEYS, PEER_HALF), lambda h, r: (h, 0, 0))],
        out_specs=[out_spec] * 4,
        out_shape=[out] * 4,
        scratch_shapes=[pltpu.VMEM((2, N_KEYS * rblk, LANES), F32),
                        pltpu.VMEM((N_KEYS * rblk, LANES), F32),
                        pltpu.VMEM((N_KEYS * rblk, LANES), F32),
                        pltpu.VMEM((2, PEER_TOPK) + tile, F32),
                        pltpu.VMEM((PEER_TOPK,) + tile, F32),
                        pltpu.VMEM((len(_PEER_CAND),) + tile, F32),
                        pltpu.VMEM((PEER_TOPK,) + tile, F32)],
        compiler_params=_params("parallel", "parallel"),
        name="peer_select",
    )(q, keys_bf16)


_DENSE_CHUNK_KEYS = 32
_DENSE_GROUP_KEYS1 = 4
_DENSE_ROWS_PER_DOT = 512


def _gelu(x):
    return 0.5 * x * (1.0 + lax.erf(x * (2.0 ** -0.5)))


def _peer_dense_stages(ht_ref, u_ref, vt_ref, e0_ref, li_ref, e1_ref, rb_ref, o_ref,
                       a_wr, a_rd, w_wr, w_rd, *, ib):
    tn = ht_ref.shape[1]

    for r0 in range(0, ib * N_KEYS, _DENSE_ROWS_PER_DOT):
        rs = slice(r0, r0 + _DENSE_ROWS_PER_DOT)
        a_wr[rs, :] = jnp.dot(u_ref[rs, :], ht_ref[...], preferred_element_type=F32)

    jc = _DENSE_CHUNK_KEYS
    rt = tn // LANES
    for i0 in range(0, ib, _DENSE_GROUP_KEYS1):
        iis = range(i0, i0 + _DENSE_GROUP_KEYS1)
        for j0 in range(0, N_KEYS, jc):
            for c in range(rt):
                lanes = slice(c * LANES, (c + 1) * LANES)
                js = pl.ds(j0 * rt + c, jc, stride=rt)
                g = [None] * len(iis)
                for h in range(PEER_HEADS):
                    rb = rb_ref[h, 0, js, :]
                    e1 = e1_ref[h, 0, js, :]
                    for n, ii in enumerate(iis):
                        first = slice(ii * rt + c, ii * rt + c + 1)
                        e0 = e0_ref[h, 0, first, :]
                        li = li_ref[h, 0, first, :]
                        t = jnp.where(rb < li, e0 * e1, 0.0)
                        g[n] = t if g[n] is None else g[n] + t
                for n, ii in enumerate(iis):
                    rows = slice(ii * N_KEYS + j0, ii * N_KEYS + j0 + jc)
                    w_wr[rows, lanes] = (_gelu(a_rd[rows, lanes]) * g[n]).astype(w_wr.dtype)

    for r0 in range(0, vt_ref.shape[0], 2 * _DENSE_ROWS_PER_DOT):
        rs = slice(r0, r0 + 2 * _DENSE_ROWS_PER_DOT)
        o_ref[rs, :] += jnp.dot(vt_ref[rs, :], w_rd[...], preferred_element_type=F32)


def _peer_dense_kernel(ht_ref, u_ref, vt_ref, e0_ref, li_ref, e1_ref, rb_ref, o_ref,
                       a0_ref, a1_ref, w0_ref, w1_ref, *, ib, n_eb):
    s = pl.program_id(0)

    @pl.when(s == 0)
    def _():
        for ref in (a0_ref, a1_ref, w0_ref, w1_ref):
            ref[...] = jnp.zeros_like(ref)

    @pl.when((s < 2) | ((s - 2) % n_eb == 0))
    def _():
        o_ref[...] = jnp.zeros_like(o_ref)

    io = (ht_ref, u_ref, vt_ref, e0_ref, li_ref, e1_ref, rb_ref, o_ref)

    @pl.when(s % 2 == 0)
    def _():
        _peer_dense_stages(*io, a0_ref, a1_ref, w1_ref, w0_ref, ib=ib)

    @pl.when(s % 2 == 1)
    def _():
        _peer_dense_stages(*io, a1_ref, a0_ref, w0_ref, w1_ref, ib=ib)


def peer_dense(h_t, u_bf16, vt_bf16, e0, li, e1, rb, *, tn, ib=8):
    d, ntok = h_t.shape
    rt = tn // LANES
    n_nb = ntok // tn
    ebs = ib * N_KEYS
    assert e1.shape == (PEER_HEADS, n_nb, N_KEYS * rt, LANES) and ntok == n_nb * tn
    assert N_KEYS % ib == 0 and ib % _DENSE_GROUP_KEYS1 == 0
    n_eb = N_KEYS // ib
    total = n_nb * n_eb

    def stage(lag):
        return lambda s: jnp.clip(s - lag, 0, total - 1)

    fa, fb, fc = stage(0), stage(1), stage(2)
    return pl.pallas_call(
        functools.partial(_peer_dense_kernel, ib=ib, n_eb=n_eb),
        grid=(total + 2,),
        in_specs=[pl.BlockSpec((d, tn), lambda s: (0, fa(s) // n_eb)),
                  pl.BlockSpec((ebs, d), lambda s: (fa(s) % n_eb, 0)),
                  pl.BlockSpec((d, ebs), lambda s: (0, fc(s) % n_eb)),
                  pl.BlockSpec((PEER_HEADS, 1, ib * rt, LANES),
                               lambda s: (0, fb(s) // n_eb, fb(s) % n_eb, 0)),
                  pl.BlockSpec((PEER_HEADS, 1, ib * rt, LANES),
                               lambda s: (0, fb(s) // n_eb, fb(s) % n_eb, 0)),
                  pl.BlockSpec((PEER_HEADS, 1, N_KEYS * rt, LANES),
                               lambda s: (0, fb(s) // n_eb, 0, 0)),
                  pl.BlockSpec((PEER_HEADS, 1, N_KEYS * rt, LANES),
                               lambda s: (0, fb(s) // n_eb, 0, 0))],
        out_specs=pl.BlockSpec((d, tn), lambda s: (0, fc(s) // n_eb)),
        out_shape=jax.ShapeDtypeStruct((d, ntok), F32),
        scratch_shapes=[pltpu.VMEM((ebs, tn), F32), pltpu.VMEM((ebs, tn), F32),
                        pltpu.VMEM((ebs, tn), BF16), pltpu.VMEM((ebs, tn), BF16)],
        compiler_params=_params("arbitrary"),
        name="peer_dense",
    )(h_t, u_bf16, vt_bf16, e0, li, e1, rb)


def _table_kernel(x_ref, o_ref, *, transposed):
    x = x_ref[...]
    o_ref[...] = (jnp.transpose(x) if transposed else x).astype(o_ref.dtype)


def expert_table_bf16(tabs, layer, *, transposed, rows=512):
    _, e, d = tabs.shape
    assert e % rows == 0
    if transposed:
        out_shape, out_spec = (d, e), pl.BlockSpec((d, rows), lambda i: (0, i))
    else:
        out_shape, out_spec = (e, d), pl.BlockSpec((rows, d), lambda i: (i, 0))
    return pl.pallas_call(
        functools.partial(_table_kernel, transposed=transposed),
        grid=(e // rows,),
        in_specs=[pl.BlockSpec((None, rows, d), lambda i: (layer, i, 0))],
        out_specs=out_spec,
        out_shape=jax.ShapeDtypeStruct(out_shape, BF16),
        compiler_params=_params("parallel"),
        name="expert_table",
    )(tabs)


def _add_transposed_kernel(x_ref, yt_ref, o_ref):
    o_ref[...] = x_ref[...] + jnp.transpose(yt_ref[...])


def add_transposed(x, y_t, *, tm):
    n, d = x.shape
    return pl.pallas_call(
        _add_transposed_kernel,
        grid=(n // tm,),
        in_specs=[pl.BlockSpec((tm, d), lambda i: (i, 0)), pl.BlockSpec((d, tm), lambda i: (0, i))],
        out_specs=pl.BlockSpec((tm, d), lambda i: (i, 0)),
        out_shape=jax.ShapeDtypeStruct((n, d), F32),
        compiler_params=_params("parallel"),
        name="add_transposed",
    )(x, y_t)


_TM = (2048, DEC_BATCH)
_TB = (512, DEC_BATCH)


def peer_layer(xs, norm_g, w_q, keys, u_tabs, v_tabs, layer):
    w_q = w_q.astype(BF16)
    keys_bf16 = keys.reshape(2 * PEER_HEADS, N_KEYS, PEER_HALF).astype(BF16)
    u_bf16 = expert_table_bf16(u_tabs, layer, transposed=False)
    vt_bf16 = expert_table_bf16(v_tabs, layer, transposed=True)
    out = []
    for x, tm, tb in zip(xs, _TM, _TB):
        h, h_t = rmsnorm(x, norm_g, BF16, tm=tb, transposed=True)
        q = matmul([h], [w_q], tm=tm, tn=512)
        rows = x.shape[0] // LANES
        sel = peer_select(q, keys_bf16, rows=rows, rblk=min(rows, 32), rt=tb // LANES)
        y_t = peer_dense(h_t, u_bf16, vt_bf16, *sel, tn=tb)
        out.append(add_transposed(x, y_t, tm=tb))
    return out


def kernel(x_prompt, x_sample, cache_mem_k, cache_mem_v, state_pool, state_gla, mem_prompt,
           norm_mix, norm_ffn, norm_final, mem_norm, w_mem_kv,
           pool_w_in, pool_w_group, pool_scale, pool_w_out,
           gla_w_in, gla_w_gate, gla_b_gate, gla_norm, gla_w_out,
           peer_w_q, peer_keys, peer_u, peer_v):
    d = D_MODEL
    xs = [x_prompt.reshape(N_PROMPT, d), x_sample.reshape(DEC_BATCH, d)]

    def project(xs, norm_g, w):
        return [matmul([rmsnorm(x, norm_g, BF16, tm=tb)], [w], tm=tm, tn=512)
                for x, tm, tb in zip(xs, _TM, _TB)]

    def mix_out(xs, mixes, atts, w, split):
        return [matmul([m, a], [w[:split], w[split:]], x, tm=tm // 2, tn=512)
                for x, m, a, tm in zip(xs, mixes, atts, _TM)]

    mem = mem_prompt.reshape(BATCH * MEM_TOKENS, d)
    kvs = []
    for i in range(DEPTH):
        mn = rmsnorm(mem, mem_norm[i], BF16, tm=512)
        kvs.append(matmul([mn], [w_mem_kv[i].astype(BF16)], tm=512, tn=512))

    def split_kv(kv, lo):
        return kv[:, lo:lo + MEM_WIDTH].reshape(BATCH, MEM_TOKENS, MEM_HEADS, MEM_HEAD_DIM)

    mem_k_prompt = jnp.stack([split_kv(kv, 0) for kv in kvs])
    mem_v_prompt = jnp.stack([split_kv(kv, MEM_WIDTH) for kv in kvs])

    u_p, u_s = project(xs, norm_mix[0], pool_w_in[0].astype(BF16))
    p_s = u_s[:, :POOL_WIDTH]
    mixes = [pool_prompt(u_p, pool_w_group[0], pool_scale[0]),
             pool_sample(jnp.transpose(state_pool[0], (1, 0, 2)), p_s, pool_w_group[0], pool_scale[0])]
    atts = [attn_prompt(u_p, POOL_WIDTH, kvs[0]),
            attn_sample(u_s[:, POOL_WIDTH:], cache_mem_k, cache_mem_v, 0)]
    xs = mix_out(xs, mixes, atts, pool_w_out[0].astype(BF16), POOL_WIDTH)
    pool_prompt_state = u_p[:, :POOL_WIDTH].reshape(BATCH, SEQ, POOL_WIDTH)[:, SEQ - POOL_BUF:][None]
    pool_sample_state = jnp.concatenate([state_pool[0][:, 1:], p_s[:, None, :]], axis=1)[None]
    xs = peer_layer(xs, norm_ffn[0], peer_w_q[0], peer_keys[0], peer_u, peer_v, 0)

    w_in = gla_w_in[0]
    o3 = 2 * GLA_KEY + GLA_VAL
    o4 = o3 + GATE_RANK
    o5 = o4 + GLA_VAL
    w_in = jnp.concatenate([
        w_in[:, :o3], w_in[:, o4:o5], w_in[:, o5:], w_in[:, o3:o4],
        jnp.zeros((d, GLA_Z_PAD - GATE_RANK), w_in.dtype)], axis=1).astype(BF16)
    wg_pad = jnp.concatenate(
        [gla_w_gate[0], jnp.zeros((LANES - GATE_RANK, GLA_KEY), gla_w_gate.dtype)], axis=0).astype(BF16)
    u_p, u_s = project(xs, norm_mix[1], w_in)
    mix_p, st_t = gla_prompt(u_p, wg_pad, gla_b_gate[0], gla_norm[0])
    mix_s, gla_s = gla_sample(u_s, state_gla[0], wg_pad, gla_b_gate[0], gla_norm[0])
    atts = [attn_prompt(u_p, GLA_COL_QM, kvs[1]),
            attn_sample(u_s[:, GLA_COL_QM:GLA_COL_QM + MEM_WIDTH], cache_mem_k, cache_mem_v, 1)]
    xs = mix_out(xs, [mix_p, mix_s.astype(BF16)], atts, gla_w_out[0].astype(BF16), GLA_VAL)
    xs = peer_layer(xs, norm_ffn[1], peer_w_q[1], peer_keys[1], peer_u, peer_v, 1)

    y_prompt, y_sample = [rmsnorm(x, norm_final, F32, tm=tb) for x, tb in zip(xs, _TB)]
    return (y_prompt.reshape(BATCH, SEQ, d),
            y_sample.reshape(DEC_BATCH, 1, d),
            pool_prompt_state,
            jnp.swapaxes(st_t, -1, -2)[None],
            mem_k_prompt,
            mem_v_prompt,
            pool_sample_state,
            gla_s[None])
```

```python
import functools

import jax
import jax.numpy as jnp
from jax import lax
from jax.experimental import pallas as pl
from jax.experimental.pallas import tpu as pltpu

D_MODEL = 2048
BATCH = 4
SEQ = 2048
DEPTH = 2
DEC_BATCH = 128
PAST_LEN = 16384
EPS = 1e-6

N_PROMPT = BATCH * SEQ
N_TOK = N_PROMPT + DEC_BATCH

POOL_WIDTH = D_MODEL // 2
POOL_GROUPS = 4
POOL_GC = POOL_WIDTH // POOL_GROUPS
POOL_WINDOWS = (2, 4, 8, 16)
POOL_BUF = max(POOL_WINDOWS) - 1

MEM_TOKENS = 256
MEM_HEADS = 4
MEM_HEAD_DIM = D_MODEL // 8
MEM_WIDTH = MEM_HEADS * MEM_HEAD_DIM
MEM_SCALE = MEM_HEAD_DIM ** -0.5

GLA_HEADS = 4
GLA_KEY = D_MODEL // 2
GLA_VAL = D_MODEL
GLA_DK = GLA_KEY // GLA_HEADS
GLA_DV = GLA_VAL // GLA_HEADS
GATE_RANK = 16
GATE_TAU = 16.0
GLA_CHUNK = 64
GLA_SCALE = GLA_DK ** -0.5

N_KEYS = 128
N_EXPERTS = N_KEYS * N_KEYS
PEER_HEADS = 8
PEER_HALF = 128
PEER_TOPK = 16

LANES = 128
SUBLANES = 8
VMEM_LIMIT = 56 * 1024 * 1024

GLA_COL_Q = 0
GLA_COL_K = GLA_KEY
GLA_COL_V = 2 * GLA_KEY
GLA_COL_OG = 2 * GLA_KEY + GLA_VAL
GLA_COL_QM = 2 * GLA_KEY + 2 * GLA_VAL
GLA_COL_Z = GLA_COL_QM + MEM_WIDTH
GLA_Z_PAD = 512
GLA_IN_COLS = GLA_COL_Z + GLA_Z_PAD

BF16 = jnp.bfloat16
F32 = jnp.float32

_NT = (((1,), (1,)), ((), ()))
_TN = (((0,), (0,)), ((), ()))


def _params(*sem):
    return pltpu.CompilerParams(dimension_semantics=sem, vmem_limit_bytes=VMEM_LIMIT)


def _rmsnorm_kernel(x_ref, g_ref, *o_refs):
    x = x_ref[...]
    y = x * lax.rsqrt(jnp.mean(x * x, axis=-1, keepdims=True) + EPS) * g_ref[...]
    o_refs[0][...] = y.astype(o_refs[0].dtype)
    if len(o_refs) > 1:
        o_refs[1][...] = jnp.transpose(y).astype(o_refs[1].dtype)


def rmsnorm(x, g, out_dtype, *, tm, transposed=False):
    rows, d = x.shape
    assert rows % tm == 0
    out_shape = [jax.ShapeDtypeStruct((rows, d), out_dtype)]
    out_specs = [pl.BlockSpec((tm, d), lambda i: (i, 0))]
    if transposed:
        out_shape.append(jax.ShapeDtypeStruct((d, rows), out_dtype))
        out_specs.append(pl.BlockSpec((d, tm), lambda i: (0, i)))
    res = pl.pallas_call(
        _rmsnorm_kernel,
        grid=(rows // tm,),
        in_specs=[pl.BlockSpec((tm, d), lambda i: (i, 0)),
                  pl.BlockSpec((1, d), lambda i: (0, 0))],
        out_specs=out_specs,
        out_shape=out_shape,
        compiler_params=_params("parallel"),
        name="rmsnorm",
    )(x, g.reshape(1, d))
    return res if transposed else res[0]


def _matmul_kernel(*refs, n_pairs, has_res):
    o_ref = refs[-1]
    acc = None
    for a_ref, w_ref in zip(refs[:n_pairs], refs[n_pairs:2 * n_pairs]):
        p = jnp.dot(a_ref[...], w_ref[...], preferred_element_type=F32)
        acc = p if acc is None else acc + p
    if has_res:
        acc = acc + refs[2 * n_pairs][...]
    o_ref[...] = acc.astype(o_ref.dtype)


def matmul(a_list, w_list, residual=None, *, tm, tn):
    n = a_list[0].shape[0]
    f = w_list[0].shape[1]
    assert n % tm == 0 and f % tn == 0
    in_specs = [pl.BlockSpec((tm, a.shape[1]), lambda i, j: (i, 0)) for a in a_list]
    in_specs += [pl.BlockSpec((w.shape[0], tn), lambda i, j: (0, j)) for w in w_list]
    args = list(a_list) + list(w_list)
    if residual is not None:
        in_specs.append(pl.BlockSpec((tm, tn), lambda i, j: (i, j)))
        args.append(residual)
    return pl.pallas_call(
        functools.partial(_matmul_kernel, n_pairs=len(a_list), has_res=residual is not None),
        grid=(n // tm, f // tn),
        in_specs=in_specs,
        out_specs=pl.BlockSpec((tm, tn), lambda i, j: (i, j)),
        out_shape=jax.ShapeDtypeStruct((n, f), F32),
        compiler_params=_params("parallel", "parallel"),
        name="matmul",
    )(*args)


def _pool_prompt_kernel(p_ref, wg_ref, sc_ref, o_ref):
    g = pl.program_id(1)
    x = p_ref[...]
    t_len = x.shape[0]
    row = lax.broadcasted_iota(jnp.int32, (t_len, 1), 0)

    def shifted(v, k):
        return jnp.where(row >= k, pltpu.roll(v, k, 0), 0.0)

    for gi, w in enumerate(POOL_WINDOWS):
        @pl.when(g == gi)
        def _(w=w):
            s = x
            k = 1
            while k < w:
                s = s + shifted(s, k)
                k *= 2
            cnt = jnp.minimum(w, row + 1).astype(F32)
            d = s / cnt - x
            out = jnp.dot(d.astype(BF16), wg_ref[0], preferred_element_type=F32)
            o_ref[...] = (out * sc_ref[0]).astype(o_ref.dtype)


def pool_prompt(u, w_group, scale):
    return pl.pallas_call(
        _pool_prompt_kernel,
        grid=(BATCH, POOL_GROUPS),
        in_specs=[pl.BlockSpec((SEQ, POOL_GC), lambda b, g: (b, g)),
                  pl.BlockSpec((1, POOL_GC, POOL_GC), lambda b, g: (g, 0, 0)),
                  pl.BlockSpec((1, 1, POOL_GC), lambda b, g: (g, 0, 0))],
        out_specs=pl.BlockSpec((SEQ, POOL_GC), lambda b, g: (b, g)),
        out_shape=jax.ShapeDtypeStruct((N_PROMPT, POOL_WIDTH), BF16),
        compiler_params=_params("parallel", "parallel"),
        name="pool_prompt",
    )(u, w_group.astype(BF16), scale.reshape(POOL_GROUPS, 1, POOL_GC))


def _pool_sample_kernel(prev_ref, p_ref, wg_ref, sc_ref, o_ref):
    g = pl.program_id(0)
    x = p_ref[...]
    for gi, w in enumerate(POOL_WINDOWS):
        @pl.when(g == gi)
        def _(w=w):
            s = x
            for r in range(POOL_BUF - (w - 1), POOL_BUF):
                s = s + prev_ref[r]
            cnt = float(min(w, PAST_LEN + 1))
            d = s / cnt - x
            out = jnp.dot(d.astype(BF16), wg_ref[0], preferred_element_type=F32)
            o_ref[...] = (out * sc_ref[0]).astype(o_ref.dtype)


def pool_sample(prev_t, p, w_group, scale):
    b = p.shape[0]
    return pl.pallas_call(
        _pool_sample_kernel,
        grid=(POOL_GROUPS,),
        in_specs=[pl.BlockSpec((POOL_BUF, b, POOL_GC), lambda g: (0, 0, g)),
                  pl.BlockSpec((b, POOL_GC), lambda g: (0, g)),
                  pl.BlockSpec((1, POOL_GC, POOL_GC), lambda g: (g, 0, 0)),
                  pl.BlockSpec((1, 1, POOL_GC), lambda g: (g, 0, 0))],
        out_specs=pl.BlockSpec((b, POOL_GC), lambda g: (0, g)),
        out_shape=jax.ShapeDtypeStruct((b, POOL_WIDTH), BF16),
        compiler_params=_params("parallel"),
        name="pool_sample",
    )(prev_t, p, w_group.astype(BF16), scale.reshape(POOL_GROUPS, 1, POOL_GC))


def _attn_prompt_kernel(q_ref, kv_ref, o_ref):
    for h in range(MEM_HEADS):
        lo, hi = h * MEM_HEAD_DIM, (h + 1) * MEM_HEAD_DIM
        qh = q_ref[:, lo:hi].astype(BF16)
        kh = kv_ref[:, lo:hi].astype(BF16)
        vh = kv_ref[:, MEM_WIDTH + lo:MEM_WIDTH + hi].astype(BF16)
        s = lax.dot_general(qh, kh, _NT, preferred_element_type=F32) * MEM_SCALE
        e = jnp.exp(s - jnp.max(s, axis=-1, keepdims=True))
        l = jnp.sum(e, axis=-1, keepdims=True)
        o = jnp.dot(e.astype(BF16), vh, preferred_element_type=F32) / l
        o_ref[:, lo:hi] = o.astype(o_ref.dtype)


def attn_prompt(u, q_col, kv, *, tq=512):
    assert q_col % MEM_WIDTH == 0
    nt = SEQ // tq
    return pl.pallas_call(
        _attn_prompt_kernel,
        grid=(BATCH, nt),
        in_specs=[pl.BlockSpec((tq, MEM_WIDTH), lambda b, t: (b * nt + t, q_col // MEM_WIDTH)),
                  pl.BlockSpec((MEM_TOKENS, 2 * MEM_WIDTH), lambda b, t: (b, 0))],
        out_specs=pl.BlockSpec((tq, MEM_WIDTH), lambda b, t: (b * nt + t, 0)),
        out_shape=jax.ShapeDtypeStruct((N_PROMPT, MEM_WIDTH), BF16),
        compiler_params=_params("parallel", "parallel"),
        name="attn_prompt",
    )(u, kv)


def _attn_sample_kernel(q_ref, k_ref, v_ref, o_ref, acc_ref):
    i = pl.program_id(0)
    bb = k_ref.shape[0]
    for n in range(bb):
        tok = pl.ds(i * bb + n, 1)
        for h in range(MEM_HEADS):
            lo, hi = h * MEM_HEAD_DIM, (h + 1) * MEM_HEAD_DIM
            s = jnp.sum(k_ref[n, :, h, :] * q_ref[tok, lo:hi], axis=-1, keepdims=True) * MEM_SCALE
            e = jnp.exp(s - jnp.max(s, axis=0, keepdims=True))
            p = e / jnp.sum(e, axis=0, keepdims=True)
            acc_ref[tok, lo:hi] = jnp.sum(p * v_ref[n, :, h, :], axis=0, keepdims=True)

    @pl.when(i == pl.num_programs(0) - 1)
    def _():
        o_ref[...] = acc_ref[...].astype(o_ref.dtype)


def attn_sample(q, cache_k, cache_v, layer, *, bb=4):
    b = q.shape[0]
    cache_spec = pl.BlockSpec((None, bb, MEM_TOKENS, MEM_HEADS, MEM_HEAD_DIM),
                              lambda i: (layer, i, 0, 0, 0))
    return pl.pallas_call(
        _attn_sample_kernel,
        grid=(b // bb,),
        in_specs=[pl.BlockSpec((b, MEM_WIDTH), lambda i: (0, 0)), cache_spec, cache_spec],
        out_specs=pl.BlockSpec((b, MEM_WIDTH), lambda i: (0, 0)),
        out_shape=jax.ShapeDtypeStruct((b, MEM_WIDTH), BF16),
        scratch_shapes=[pltpu.VMEM((b, MEM_WIDTH), F32)],
        compiler_params=_params("arbitrary"),
        name="attn_sample",
    )(q, cache_k, cache_v)


def _log_decay(z, wg_ref, bg_ref):
    g = jnp.dot(z.astype(BF16), wg_ref[...], preferred_element_type=F32) + bg_ref[...]
    return jax.nn.log_sigmoid(g) / GATE_TAU


def _gla_out(o, og, gn):
    on = o * lax.rsqrt(jnp.mean(o * o, axis=-1, keepdims=True) + EPS) * gn
    return on * (og * jax.nn.sigmoid(og))


def _gla_prompt_kernel(q_ref, k_ref, v_ref, og_ref, z_ref, wg_ref, bg_ref, gn_ref,
                       mix_ref, st_ref, s_ref):
    n = pl.program_id(1)
    c = q_ref.shape[0]

    @pl.when(n == 0)
    def _():
        s_ref[...] = jnp.zeros_like(s_ref)

    la = _log_decay(z_ref[...], wg_ref, bg_ref)
    row = lax.broadcasted_iota(jnp.int32, (c, 1), 0)
    b = la
    k = 1
    while k < c:
        b = b + jnp.where(row >= k, pltpu.roll(b, k, 0), 0.0)
        k *= 2
    b_last = b[c - 1:c, :]
    q_in = q_ref[...] * jnp.exp(b) * GLA_SCALE
    k_in = k_ref[...] * jnp.exp(-b)
    k_out = k_ref[...] * jnp.exp(b_last - b)
    decay = jnp.exp(b_last)
    causal = row >= lax.broadcasted_iota(jnp.int32, (1, c), 1)
    gn = gn_ref[...]
    for h in range(GLA_HEADS):
        klo, khi = h * GLA_DK, (h + 1) * GLA_DK
        vlo, vhi = h * GLA_DV, (h + 1) * GLA_DV
        qh = q_in[:, klo:khi].astype(BF16)
        vh = v_ref[:, vlo:vhi].astype(BF16)
        a = lax.dot_general(qh, k_in[:, klo:khi].astype(BF16), _NT, preferred_element_type=F32)
        a = jnp.where(causal, a, 0.0)
        st = s_ref[h]
        o = (jnp.dot(a.astype(BF16), vh, preferred_element_type=F32)
             + lax.dot_general(qh, st.astype(BF16), _NT, preferred_element_type=F32))
        s_ref[h] = st * decay[:, klo:khi] + lax.dot_general(
            vh, k_out[:, klo:khi].astype(BF16), _TN, preferred_element_type=F32)
        mix_ref[:, vlo:vhi] = _gla_out(o, og_ref[:, vlo:vhi], gn).astype(mix_ref.dtype)

    @pl.when(n == pl.num_programs(1) - 1)
    def _():
        st_ref[0] = s_ref[...]


def gla_prompt(u, wg_pad, b_gate, norm_g):
    nc = SEQ // GLA_CHUNK
    c = GLA_CHUNK

    def col(width, off):
        assert off % width == 0
        return pl.BlockSpec((c, width), lambda b, n: (b * nc + n, off // width))

    return pl.pallas_call(
        _gla_prompt_kernel,
        grid=(BATCH, nc),
        in_specs=[col(GLA_KEY, GLA_COL_Q), col(GLA_KEY, GLA_COL_K), col(GLA_VAL, GLA_COL_V),
                  col(GLA_VAL, GLA_COL_OG), col(LANES, GLA_COL_Z),
                  pl.BlockSpec((LANES, GLA_KEY), lambda b, n: (0, 0)),
                  pl.BlockSpec((1, GLA_KEY), lambda b, n: (0, 0)),
                  pl.BlockSpec((1, GLA_DV), lambda b, n: (0, 0))],
        out_specs=[pl.BlockSpec((c, GLA_VAL), lambda b, n: (b * nc + n, 0)),
                   pl.BlockSpec((1, GLA_HEADS, GLA_DV, GLA_DK), lambda b, n: (b, 0, 0, 0))],
        out_shape=[jax.ShapeDtypeStruct((N_PROMPT, GLA_VAL), BF16),
                   jax.ShapeDtypeStruct((BATCH, GLA_HEADS, GLA_DV, GLA_DK), F32)],
        scratch_shapes=[pltpu.VMEM((GLA_HEADS, GLA_DV, GLA_DK), F32)],
        compiler_params=_params("parallel", "arbitrary"),
        name="gla_prompt",
    )(u, u, u, u, u, wg_pad, b_gate.reshape(1, GLA_KEY), norm_g.reshape(1, GLA_DV))


def _gla_sample_kernel(us_ref, s0_ref, wg_ref, bg_ref, gn_ref, mix_ref, s1_ref, la_ref):
    i = pl.program_id(0)

    @pl.when(i == 0)
    def _():
        la_ref[...] = _log_decay(us_ref[:, GLA_COL_Z:GLA_COL_Z + LANES], wg_ref, bg_ref)

    tok = pl.ds(i, 1)
    gn = gn_ref[...]
    for h in range(GLA_HEADS):
        klo, khi = h * GLA_DK, (h + 1) * GLA_DK
        vlo, vhi = h * GLA_DV, (h + 1) * GLA_DV
        la = la_ref[tok, klo:khi]
        q = us_ref[tok, GLA_COL_Q + klo:GLA_COL_Q + khi]
        k = us_ref[tok, GLA_COL_K + klo:GLA_COL_K + khi]
        v = us_ref[tok, GLA_COL_V + vlo:GLA_COL_V + vhi]
        og = us_ref[tok, GLA_COL_OG + vlo:GLA_COL_OG + vhi]
        q_in = q * jnp.exp(la) * GLA_SCALE
        k_in = k * jnp.exp(-la)
        k_out = k * jnp.exp(la - la)
        decay = jnp.exp(la)
        a = jnp.sum(q_in * k_in, axis=-1, keepdims=True)
        rows = jnp.concatenate(
            [q_in, k_out, decay, jnp.zeros((SUBLANES - 3, GLA_DK), F32)], axis=0)
        cols = jnp.transpose(rows)
        s0 = s0_ref[0, h]
        o = a * v + jnp.sum(cols[:, 0:1] * s0, axis=0, keepdims=True)
        s1_ref[0, h] = s0 * cols[:, 2:3] + cols[:, 1:2] * v
        mix_ref[tok, vlo:vhi] = _gla_out(o, og, gn).astype(mix_ref.dtype)


def gla_sample(us, s0, wg_pad, b_gate, norm_g):
    b = us.shape[0]
    return pl.pallas_call(
        _gla_sample_kernel,
        grid=(b,),
        in_specs=[pl.BlockSpec((b, GLA_IN_COLS), lambda i: (0, 0)),
                  pl.BlockSpec((1, GLA_HEADS, GLA_DK, GLA_DV), lambda i: (i, 0, 0, 0)),
                  pl.BlockSpec((LANES, GLA_KEY), lambda i: (0, 0)),
                  pl.BlockSpec((1, GLA_KEY), lambda i: (0, 0)),
                  pl.BlockSpec((1, GLA_DV), lambda i: (0, 0))],
        out_specs=[pl.BlockSpec((b, GLA_VAL), lambda i: (0, 0)),
                   pl.BlockSpec((1, GLA_HEADS, GLA_DK, GLA_DV), lambda i: (i, 0, 0, 0))],
        out_shape=[jax.ShapeDtypeStruct((b, GLA_VAL), F32),
                   jax.ShapeDtypeStruct(s0.shape, F32)],
        scratch_shapes=[pltpu.VMEM((b, GLA_KEY), F32)],
        compiler_params=_params("arbitrary"),
        name="gla_sample",
    )(us, s0, wg_pad, b_gate.reshape(1, GLA_KEY), norm_g.reshape(1, GLA_DV))


_PEER_CAND = tuple((a, b) for a in range(PEER_TOPK) for b in range(PEER_TOPK)
                   if (a + 1) * (b + 1) <= PEER_TOPK)
_NEG_INF = float("-inf")
_UNRANKED = float(N_KEYS)


_REDUCE_WIDTH = 8


def _reduce(op, vals):
    accs = list(vals[:_REDUCE_WIDTH])
    for n, v in enumerate(vals[_REDUCE_WIDTH:]):
        accs[n % _REDUCE_WIDTH] = op(accs[n % _REDUCE_WIDTH], v)
    while len(accs) > 1:
        accs = [op(accs[n], accs[n + 1]) if n + 1 < len(accs) else accs[n]
                for n in range(0, len(accs), 2)]
    return accs[0]


def _peer_select_kernel(q_ref, keys_ref, e0_ref, li_ref, e1_ref, rb_ref,
                        s_ref, cur_ref, rank_ref, sv_ref, ix_ref, cand_ref, cnt_ref,
                        *, rt, rows_are_heads):
    n_rows = s_ref.shape[1] // N_KEYS
    tile = (n_rows, LANES)

    def key_rows(k):
        return slice(k * n_rows, (k + 1) * n_rows)

    for p in range(2):
        for r in range(n_rows):
            if rows_are_heads:
                col0 = (2 * r + p) * PEER_HALF
                qs, kk = q_ref[:, col0:col0 + PEER_HALF], keys_ref[2 * r + p]
            else:
                qs = q_ref[r * LANES:(r + 1) * LANES, p * PEER_HALF:(p + 1) * PEER_HALF]
                kk = keys_ref[p]
            s_ref[p, pl.ds(r, N_KEYS, stride=n_rows), :] = lax.dot_general(
                kk, qs.astype(BF16), _NT, preferred_element_type=F32)

    for p in range(2):
        top = _reduce(jnp.maximum, [s_ref[p, key_rows(k)] for k in range(N_KEYS)])

        def start(p=p):
            cur_ref[...] = s_ref[p]
            if p == 1:
                rank_ref[...] = jnp.full(rank_ref.shape, _UNRANKED, F32)

        def extract(r, m, p=p, one_key=False):
            sv_ref[p, pl.ds(r, 1)] = m[None]
            rf = lax.convert_element_type(r, F32)
            if one_key:
                idx = _reduce(jnp.minimum,
                              [jnp.where(cur_ref[key_rows(k)] == m, float(k), _UNRANKED)
                               for k in range(N_KEYS)])
            else:
                idx = jnp.full(tile, _UNRANKED, F32)
            rest = []
            for k in range(N_KEYS):
                old = cur_ref[key_rows(k)]
                hit = (idx == float(k)) if one_key else (old == m)
                c = jnp.where(hit, _NEG_INF, old)
                cur_ref[key_rows(k)] = c
                if p == 1:
                    rank_ref[key_rows(k)] = jnp.where(hit, rf, rank_ref[key_rows(k)])
                elif not one_key:
                    idx = jnp.where(hit, float(k), idx)
                rest.append(c)
            if p == 0:
                ix_ref[pl.ds(r, 1)] = idx[None]
            return _reduce(jnp.maximum, rest)

        start()
        lax.fori_loop(0, PEER_TOPK, extract, top)
        removed = _reduce(jnp.add, [jnp.where(cur_ref[key_rows(k)] == _NEG_INF, 1.0, 0.0)
                                    for k in range(N_KEYS)])

        @pl.when(jnp.max(jnp.abs(removed - float(PEER_TOPK))) > 0.0)
        def _(start=start, extract=extract, top=top):
            start()
            lax.fori_loop(0, PEER_TOPK, functools.partial(extract, one_key=True), top)

    for ci, (a, b) in enumerate(_PEER_CAND):
        cand_ref[ci] = sv_ref[0, a] + sv_ref[1, b]
    cnt_ref[...] = jnp.zeros_like(cnt_ref)
    c_max = sv_ref[0, 0] + sv_ref[1, 0]
    no_flat = float(PEER_TOPK * PEER_TOPK)

    def pick(_, z):
        m = _reduce(jnp.maximum, [cand_ref[ci] for ci in range(len(_PEER_CAND))])
        flat = _reduce(jnp.minimum, [
            jnp.where(cand_ref[ci] == m, float(a * PEER_TOPK + b), no_flat)
            for ci, (a, b) in enumerate(_PEER_CAND)])
        for ci, (a, b) in enumerate(_PEER_CAND):
            cand_ref[ci] = jnp.where(flat == float(a * PEER_TOPK + b), _NEG_INF, cand_ref[ci])
        first = jnp.floor(flat * (1.0 / PEER_TOPK))
        for a in range(PEER_TOPK):
            cnt_ref[a] = cnt_ref[a] + jnp.where(first == float(a), 1.0, 0.0)
        return z + jnp.exp(m - c_max)

    inv_z = 1.0 / lax.fori_loop(0, PEER_TOPK, pick, jnp.zeros(tile, F32))

    max0 = sv_ref[0, 0]
    max1 = sv_ref[1, 0]

    for k in range(N_KEYS):
        li = jnp.zeros(tile, F32)
        for a in range(PEER_TOPK):
            li = jnp.where(ix_ref[a] == float(k), cnt_ref[a], li)
        e0 = jnp.exp(s_ref[0, key_rows(k)] - max0) * inv_z
        e1 = jnp.exp(s_ref[1, key_rows(k)] - max1)
        rb = rank_ref[key_rows(k)]
        for b in range(n_rows // rt):
            src = slice(b * rt, (b + 1) * rt)
            dst = slice(k * rt, (k + 1) * rt)
            e0_ref[0, b, dst, :] = e0[src]
            li_ref[0, b, dst, :] = li[src]
            e1_ref[0, b, dst, :] = e1[src]
            rb_ref[0, b, dst, :] = rb[src]


def peer_select(q, keys_bf16, *, rows, rblk, rt):
    assert rows % rblk == 0 and rblk % rt == 0
    rows_are_heads = rblk == 1
    if rows_are_heads:
        n_tile = PEER_HEADS
        grid = (1, rows)
        in_specs = [pl.BlockSpec((LANES, q.shape[1]), lambda h, r: (r, 0)),
                    pl.BlockSpec(keys_bf16.shape, lambda h, r: (0, 0, 0))]
        out = jax.ShapeDtypeStruct((rows, PEER_HEADS, N_KEYS, LANES), F32)
        out_spec = pl.BlockSpec((1, PEER_HEADS, N_KEYS, LANES), lambda h, r: (r, 0, 0, 0))
    else:
        n_tile = rblk
        grid = (PEER_HEADS, rows // rblk)
        in_specs = [pl.BlockSpec((rblk * LANES, 2 * PEER_HALF), lambda h, r: (r, h)),
                    pl.BlockSpec((2, N_KEYS, PEER_HALF), lambda h, r: (h, 0, 0))]
        out = jax.ShapeDtypeStruct((PEER_HEADS, rows // rt, N_KEYS * rt, LANES), F32)
        out_spec = pl.BlockSpec((1, rblk // rt, N_KEYS * rt, LANES), lambda h, r: (h, r, 0, 0))
    tile = (n_tile, LANES)
    res = pl.pallas_call(
        functools.partial(_peer_select_kernel, rt=rt, rows_are_heads=rows_are_heads),
        grid=grid,
        in_specs=in_specs,
        out_specs=[out_spec] * 4,
        out_shape=[out] * 4,
        scratch_shapes=[pltpu.VMEM((2, N_KEYS * n_tile, LANES), F32),
                        pltpu.VMEM((N_KEYS * n_tile, LANES), F32),
                        pltpu.VMEM((N_KEYS * n_tile, LANES), F32),
                        pltpu.VMEM((2, PEER_TOPK) + tile, F32),
                        pltpu.VMEM((PEER_TOPK,) + tile, F32),
                        pltpu.VMEM((len(_PEER_CAND),) + tile, F32),
                        pltpu.VMEM((PEER_TOPK,) + tile, F32)],
        compiler_params=_params("parallel", "parallel"),
        name="peer_select",
    )(q, keys_bf16)
    return [jnp.swapaxes(r, 0, 1) for r in res] if rows_are_heads else res


_DENSE_CHUNK_KEYS = 32
_DENSE_GROUP_KEYS1 = 4
_DENSE_ROWS_PER_DOT = 512


def _gelu(x):
    return 0.5 * x * (1.0 + lax.erf(x * (2.0 ** -0.5)))


def _peer_dense_stages(ht_ref, u_ref, vt_ref, e0_ref, li_ref, e1_ref, rb_ref, o_ref,
                       a_wr, a_rd, w_wr, w_rd, *, ib):
    tn = ht_ref.shape[1]

    for r0 in range(0, ib * N_KEYS, _DENSE_ROWS_PER_DOT):
        rs = slice(r0, r0 + _DENSE_ROWS_PER_DOT)
        a_wr[rs, :] = jnp.dot(u_ref[rs, :], ht_ref[...], preferred_element_type=F32)

    jc = _DENSE_CHUNK_KEYS
    rt = tn // LANES
    for i0 in range(0, ib, _DENSE_GROUP_KEYS1):
        iis = range(i0, i0 + _DENSE_GROUP_KEYS1)
        for j0 in range(0, N_KEYS, jc):
            for c in range(rt):
                lanes = slice(c * LANES, (c + 1) * LANES)
                js = pl.ds(j0 * rt + c, jc, stride=rt)
                g = [None] * len(iis)
                for h in range(PEER_HEADS):
                    rb = rb_ref[h, 0, js, :]
                    e1 = e1_ref[h, 0, js, :]
                    for n, ii in enumerate(iis):
                        first = slice(ii * rt + c, ii * rt + c + 1)
                        e0 = e0_ref[h, 0, first, :]
                        li = li_ref[h, 0, first, :]
                        t = jnp.where(rb < li, e0 * e1, 0.0)
                        g[n] = t if g[n] is None else g[n] + t
                for n, ii in enumerate(iis):
                    rows = slice(ii * N_KEYS + j0, ii * N_KEYS + j0 + jc)
                    w_wr[rows, lanes] = (_gelu(a_rd[rows, lanes]) * g[n]).astype(w_wr.dtype)

    for r0 in range(0, vt_ref.shape[0], 2 * _DENSE_ROWS_PER_DOT):
        rs = slice(r0, r0 + 2 * _DENSE_ROWS_PER_DOT)
        o_ref[rs, :] += jnp.dot(vt_ref[rs, :], w_rd[...], preferred_element_type=F32)


def _peer_dense_kernel(ht_ref, u_ref, vt_ref, e0_ref, li_ref, e1_ref, rb_ref, o_ref,
                       a0_ref, a1_ref, w0_ref, w1_ref, *, ib, n_eb):
    s = pl.program_id(0)

    @pl.when(s == 0)
    def _():
        for ref in (a0_ref, a1_ref, w0_ref, w1_ref):
            ref[...] = jnp.zeros_like(ref)

    @pl.when((s < 2) | ((s - 2) % n_eb == 0))
    def _():
        o_ref[...] = jnp.zeros_like(o_ref)

    io = (ht_ref, u_ref, vt_ref, e0_ref, li_ref, e1_ref, rb_ref, o_ref)

    @pl.when(s % 2 == 0)
    def _():
        _peer_dense_stages(*io, a0_ref, a1_ref, w1_ref, w0_ref, ib=ib)

    @pl.when(s % 2 == 1)
    def _():
        _peer_dense_stages(*io, a1_ref, a0_ref, w0_ref, w1_ref, ib=ib)


def peer_dense(h_t, u_bf16, vt_bf16, e0, li, e1, rb, *, tn, ib=8):
    d, ntok = h_t.shape
    rt = tn // LANES
    n_nb = ntok // tn
    ebs = ib * N_KEYS
    assert e1.shape == (PEER_HEADS, n_nb, N_KEYS * rt, LANES) and ntok == n_nb * tn
    assert N_KEYS % ib == 0 and ib % _DENSE_GROUP_KEYS1 == 0
    n_eb = N_KEYS // ib
    total = n_nb * n_eb

    def stage(lag):
        return lambda s: jnp.clip(s - lag, 0, total - 1)

    fa, fb, fc = stage(0), stage(1), stage(2)
    return pl.pallas_call(
        functools.partial(_peer_dense_kernel, ib=ib, n_eb=n_eb),
        grid=(total + 2,),
        in_specs=[pl.BlockSpec((d, tn), lambda s: (0, fa(s) // n_eb)),
                  pl.BlockSpec((ebs, d), lambda s: (fa(s) % n_eb, 0)),
                  pl.BlockSpec((d, ebs), lambda s: (0, fc(s) % n_eb)),
                  pl.BlockSpec((PEER_HEADS, 1, ib * rt, LANES),
                               lambda s: (0, fb(s) // n_eb, fb(s) % n_eb, 0)),
                  pl.BlockSpec((PEER_HEADS, 1, ib * rt, LANES),
                               lambda s: (0, fb(s) // n_eb, fb(s) % n_eb, 0)),
                  pl.BlockSpec((PEER_HEADS, 1, N_KEYS * rt, LANES),
                               lambda s: (0, fb(s) // n_eb, 0, 0)),
                  pl.BlockSpec((PEER_HEADS, 1, N_KEYS * rt, LANES),
                               lambda s: (0, fb(s) // n_eb, 0, 0))],
        out_specs=pl.BlockSpec((d, tn), lambda s: (0, fc(s) // n_eb)),
        out_shape=jax.ShapeDtypeStruct((d, ntok), F32),
        scratch_shapes=[pltpu.VMEM((ebs, tn), F32), pltpu.VMEM((ebs, tn), F32),
                        pltpu.VMEM((ebs, tn), BF16), pltpu.VMEM((ebs, tn), BF16)],
        compiler_params=_params("arbitrary"),
        name="peer_dense",
    )(h_t, u_bf16, vt_bf16, e0, li, e1, rb)


def _table_kernel(x_ref, o_ref, *, transposed):
    x = x_ref[...]
    o_ref[...] = (jnp.transpose(x) if transposed else x).astype(o_ref.dtype)


def expert_table_bf16(tabs, layer, *, transposed, rows=512):
    _, e, d = tabs.shape
    assert e % rows == 0
    if transposed:
        out_shape, out_spec = (d, e), pl.BlockSpec((d, rows), lambda i: (0, i))
    else:
        out_shape, out_spec = (e, d), pl.BlockSpec((rows, d), lambda i: (i, 0))
    return pl.pallas_call(
        functools.partial(_table_kernel, transposed=transposed),
        grid=(e // rows,),
        in_specs=[pl.BlockSpec((None, rows, d), lambda i: (layer, i, 0))],
        out_specs=out_spec,
        out_shape=jax.ShapeDtypeStruct(out_shape, BF16),
        compiler_params=_params("parallel"),
        name="expert_table",
    )(tabs)


def _add_transposed_kernel(x_ref, yt_ref, o_ref):
    o_ref[...] = x_ref[...] + jnp.transpose(yt_ref[...])


def add_transposed(x, y_t, *, tm):
    n, d = x.shape
    return pl.pallas_call(
        _add_transposed_kernel,
        grid=(n // tm,),
        in_specs=[pl.BlockSpec((tm, d), lambda i: (i, 0)), pl.BlockSpec((d, tm), lambda i: (0, i))],
        out_specs=pl.BlockSpec((tm, d), lambda i: (i, 0)),
        out_shape=jax.ShapeDtypeStruct((n, d), F32),
        compiler_params=_params("parallel"),
        name="add_transposed",
    )(x, y_t)


_TM = (2048, DEC_BATCH)
_TB = (512, DEC_BATCH)


def peer_layer(xs, norm_g, w_q, keys, u_tabs, v_tabs, layer):
    w_q = w_q.astype(BF16)
    keys_bf16 = keys.reshape(2 * PEER_HEADS, N_KEYS, PEER_HALF).astype(BF16)
    u_bf16 = expert_table_bf16(u_tabs, layer, transposed=False)
    vt_bf16 = expert_table_bf16(v_tabs, layer, transposed=True)
    out = []
    for x, tm, tb in zip(xs, _TM, _TB):
        h, h_t = rmsnorm(x, norm_g, BF16, tm=tb, transposed=True)
        q = matmul([h], [w_q], tm=tm, tn=512)
        rows = x.shape[0] // LANES
        sel = peer_select(q, keys_bf16, rows=rows, rblk=min(rows, 32), rt=tb // LANES)
        y_t = peer_dense(h_t, u_bf16, vt_bf16, *sel, tn=tb)
        out.append(add_transposed(x, y_t, tm=tb))
    return out


def kernel(x_prompt, x_sample, cache_mem_k, cache_mem_v, state_pool, state_gla, mem_prompt,
           norm_mix, norm_ffn, norm_final, mem_norm, w_mem_kv,
           pool_w_in, pool_w_group, pool_scale, pool_w_out,
           gla_w_in, gla_w_gate, gla_b_gate, gla_norm, gla_w_out,
           peer_w_q, peer_keys, peer_u, peer_v):
    d = D_MODEL
    xs = [x_prompt.reshape(N_PROMPT, d), x_sample.reshape(DEC_BATCH, d)]

    def project(xs, norm_g, w):
        return [matmul([rmsnorm(x, norm_g, BF16, tm=tb)], [w], tm=tm, tn=512)
                for x, tm, tb in zip(xs, _TM, _TB)]

    def mix_out(xs, mixes, atts, w, split):
        return [matmul([m, a], [w[:split], w[split:]], x, tm=tm // 2, tn=512)
                for x, m, a, tm in zip(xs, mixes, atts, _TM)]

    mem = mem_prompt.reshape(BATCH * MEM_TOKENS, d)
    kvs = []
    for i in range(DEPTH):
        mn = rmsnorm(mem, mem_norm[i], BF16, tm=512)
        kvs.append(matmul([mn], [w_mem_kv[i].astype(BF16)], tm=512, tn=512))

    def split_kv(kv, lo):
        return kv[:, lo:lo + MEM_WIDTH].reshape(BATCH, MEM_TOKENS, MEM_HEADS, MEM_HEAD_DIM)

    mem_k_prompt = jnp.stack([split_kv(kv, 0) for kv in kvs])
    mem_v_prompt = jnp.stack([split_kv(kv, MEM_WIDTH) for kv in kvs])

    u_p, u_s = project(xs, norm_mix[0], pool_w_in[0].astype(BF16))
    p_s = u_s[:, :POOL_WIDTH]
    mixes = [pool_prompt(u_p, pool_w_group[0], pool_scale[0]),
             pool_sample(jnp.transpose(state_pool[0], (1, 0, 2)), p_s, pool_w_group[0], pool_scale[0])]
    atts = [attn_prompt(u_p, POOL_WIDTH, kvs[0]),
            attn_sample(u_s[:, POOL_WIDTH:], cache_mem_k, cache_mem_v, 0)]
    xs = mix_out(xs, mixes, atts, pool_w_out[0].astype(BF16), POOL_WIDTH)
    pool_prompt_state = u_p[:, :POOL_WIDTH].reshape(BATCH, SEQ, POOL_WIDTH)[:, SEQ - POOL_BUF:][None]
    pool_sample_state = jnp.concatenate([state_pool[0][:, 1:], p_s[:, None, :]], axis=1)[None]
    xs = peer_layer(xs, norm_ffn[0], peer_w_q[0], peer_keys[0], peer_u, peer_v, 0)

    w_in = gla_w_in[0]
    o3 = 2 * GLA_KEY + GLA_VAL
    o4 = o3 + GATE_RANK
    o5 = o4 + GLA_VAL
    w_in = jnp.concatenate([
        w_in[:, :o3], w_in[:, o4:o5], w_in[:, o5:], w_in[:, o3:o4],
        jnp.zeros((d, GLA_Z_PAD - GATE_RANK), w_in.dtype)], axis=1).astype(BF16)
    wg_pad = jnp.concatenate(
        [gla_w_gate[0], jnp.zeros((LANES - GATE_RANK, GLA_KEY), gla_w_gate.dtype)], axis=0).astype(BF16)
    u_p, u_s = project(xs, norm_mix[1], w_in)
    mix_p, st_t = gla_prompt(u_p, wg_pad, gla_b_gate[0], gla_norm[0])
    mix_s, gla_s = gla_sample(u_s, state_gla[0], wg_pad, gla_b_gate[0], gla_norm[0])
    atts = [attn_prompt(u_p, GLA_COL_QM, kvs[1]),
            attn_sample(u_s[:, GLA_COL_QM:GLA_COL_QM + MEM_WIDTH], cache_mem_k, cache_mem_v, 1)]
    xs = mix_out(xs, [mix_p, mix_s.astype(BF16)], atts, gla_w_out[0].astype(BF16), GLA_VAL)
    xs = peer_layer(xs, norm_ffn[1], peer_w_q[1], peer_keys[1], peer_u, peer_v, 1)

    y_prompt, y_sample = [rmsnorm(x, norm_final, F32, tm=tb) for x, tb in zip(xs, _TB)]
    return (y_prompt.reshape(BATCH, SEQ, d),
            y_sample.reshape(DEC_BATCH, 1, d),
            pool_prompt_state,
            jnp.swapaxes(st_t, -1, -2)[None],
            mem_k_prompt,
            mem_v_prompt,
            pool_sample_state,
            gla_s[None])
```

```python
import functools

import jax
import jax.numpy as jnp
from jax import lax
from jax.experimental import pallas as pl
from jax.experimental.pallas import tpu as pltpu

D_MODEL = 2048
BATCH = 4
SEQ = 2048
DEPTH = 2
DEC_BATCH = 128
PAST_LEN = 16384
EPS = 1e-6

N_PROMPT = BATCH * SEQ

POOL_WIDTH = D_MODEL // 2
POOL_GROUPS = 4
POOL_GC = POOL_WIDTH // POOL_GROUPS
POOL_WINDOWS = (2, 4, 8, 16)
POOL_BUF = max(POOL_WINDOWS) - 1

MEM_TOKENS = 256
MEM_HEADS = 4
MEM_HEAD_DIM = D_MODEL // 8
MEM_WIDTH = MEM_HEADS * MEM_HEAD_DIM
MEM_SCALE = MEM_HEAD_DIM ** -0.5

GLA_HEADS = 4
GLA_KEY = D_MODEL // 2
GLA_VAL = D_MODEL
GLA_DK = GLA_KEY // GLA_HEADS
GLA_DV = GLA_VAL // GLA_HEADS
GATE_RANK = 16
GATE_TAU = 16.0
GLA_CHUNK = 64
GLA_SCALE = GLA_DK ** -0.5

N_KEYS = 128
N_EXPERTS = N_KEYS * N_KEYS
PEER_HEADS = 8
PEER_HALF = 128
PEER_TOPK = 16

LANES = 128
SUBLANES = 8
VMEM_LIMIT = 56 * 1024 * 1024

GLA_COL_Q = 0
GLA_COL_K = GLA_KEY
GLA_COL_V = 2 * GLA_KEY
GLA_COL_OG = 2 * GLA_KEY + GLA_VAL
GLA_COL_QM = 2 * GLA_KEY + 2 * GLA_VAL
GLA_COL_Z = GLA_COL_QM + MEM_WIDTH
GLA_Z_PAD = 512
GLA_IN_COLS = GLA_COL_Z + GLA_Z_PAD

BF16 = jnp.bfloat16
F32 = jnp.float32

_NT = (((1,), (1,)), ((), ()))
_TN = (((0,), (0,)), ((), ()))


def _params(*sem):
    return pltpu.CompilerParams(dimension_semantics=sem, vmem_limit_bytes=VMEM_LIMIT)


def _rmsnorm_kernel(x_ref, g_ref, *o_refs):
    x = x_ref[...]
    y = x * lax.rsqrt(jnp.mean(x * x, axis=-1, keepdims=True) + EPS) * g_ref[...]
    o_refs[0][...] = y.astype(o_refs[0].dtype)
    if len(o_refs) > 1:
        o_refs[1][...] = jnp.transpose(y).astype(o_refs[1].dtype)


def rmsnorm(x, g, out_dtype, *, tm, transposed=False):
    rows, d = x.shape
    assert rows % tm == 0
    out_shape = [jax.ShapeDtypeStruct((rows, d), out_dtype)]
    out_specs = [pl.BlockSpec((tm, d), lambda i: (i, 0))]
    if transposed:
        out_shape.append(jax.ShapeDtypeStruct((d, rows), out_dtype))
        out_specs.append(pl.BlockSpec((d, tm), lambda i: (0, i)))
    res = pl.pallas_call(
        _rmsnorm_kernel,
        grid=(rows // tm,),
        in_specs=[pl.BlockSpec((tm, d), lambda i: (i, 0)),
                  pl.BlockSpec((1, d), lambda i: (0, 0))],
        out_specs=out_specs,
        out_shape=out_shape,
        compiler_params=_params("parallel"),
        name="rmsnorm",
    )(x, g.reshape(1, d))
    return res if transposed else res[0]


def _matmul_kernel(*refs, n_pairs, has_res):
    o_ref = refs[-1]
    acc = None
    for a_ref, w_ref in zip(refs[:n_pairs], refs[n_pairs:2 * n_pairs]):
        p = jnp.dot(a_ref[...], w_ref[...], preferred_element_type=F32)
        acc = p if acc is None else acc + p
    if has_res:
        acc = acc + refs[2 * n_pairs][...]
    o_ref[...] = acc.astype(o_ref.dtype)


def matmul(a_list, w_list, residual=None, *, tm, tn):
    n = a_list[0].shape[0]
    f = w_list[0].shape[1]
    assert n % tm == 0 and f % tn == 0
    in_specs = [pl.BlockSpec((tm, a.shape[1]), lambda i, j: (i, 0)) for a in a_list]
    in_specs += [pl.BlockSpec((w.shape[0], tn), lambda i, j: (0, j)) for w in w_list]
    args = list(a_list) + list(w_list)
    if residual is not None:
        in_specs.append(pl.BlockSpec((tm, tn), lambda i, j: (i, j)))
        args.append(residual)
    return pl.pallas_call(
        functools.partial(_matmul_kernel, n_pairs=len(a_list), has_res=residual is not None),
        grid=(n // tm, f // tn),
        in_specs=in_specs,
        out_specs=pl.BlockSpec((tm, tn), lambda i, j: (i, j)),
        out_shape=jax.ShapeDtypeStruct((n, f), F32),
        compiler_params=_params("parallel", "parallel"),
        name="matmul",
    )(*args)


def _pool_prompt_kernel(p_ref, wg_ref, sc_ref, o_ref):
    g = pl.program_id(1)
    x = p_ref[...]
    t_len = x.shape[0]
    row = lax.broadcasted_iota(jnp.int32, (t_len, 1), 0)

    def shifted(v, k):
        return jnp.where(row >= k, pltpu.roll(v, k, 0), 0.0)

    for gi, w in enumerate(POOL_WINDOWS):
        @pl.when(g == gi)
        def _(w=w):
            s = x
            k = 1
            while k < w:
                s = s + shifted(s, k)
                k *= 2
            cnt = jnp.minimum(w, row + 1).astype(F32)
            d = s / cnt - x
            out = jnp.dot(d.astype(BF16), wg_ref[0], preferred_element_type=F32)
            o_ref[...] = (out * sc_ref[0]).astype(o_ref.dtype)


def pool_prompt(u, w_group, scale):
    return pl.pallas_call(
        _pool_prompt_kernel,
        grid=(BATCH, POOL_GROUPS),
        in_specs=[pl.BlockSpec((SEQ, POOL_GC), lambda b, g: (b, g)),
                  pl.BlockSpec((1, POOL_GC, POOL_GC), lambda b, g: (g, 0, 0)),
                  pl.BlockSpec((1, 1, POOL_GC), lambda b, g: (g, 0, 0))],
        out_specs=pl.BlockSpec((SEQ, POOL_GC), lambda b, g: (b, g)),
        out_shape=jax.ShapeDtypeStruct((N_PROMPT, POOL_WIDTH), BF16),
        compiler_params=_params("parallel", "parallel"),
        name="pool_prompt",
    )(u, w_group.astype(BF16), scale.reshape(POOL_GROUPS, 1, POOL_GC))


def _pool_sample_kernel(prev_ref, p_ref, wg_ref, sc_ref, o_ref):
    g = pl.program_id(0)
    x = p_ref[...]
    for gi, w in enumerate(POOL_WINDOWS):
        @pl.when(g == gi)
        def _(w=w):
            s = x
            for r in range(POOL_BUF - (w - 1), POOL_BUF):
                s = s + prev_ref[r]
            cnt = float(min(w, PAST_LEN + 1))
            d = s / cnt - x
            out = jnp.dot(d.astype(BF16), wg_ref[0], preferred_element_type=F32)
            o_ref[...] = (out * sc_ref[0]).astype(o_ref.dtype)


def pool_sample(prev_t, p, w_group, scale):
    b = p.shape[0]
    return pl.pallas_call(
        _pool_sample_kernel,
        grid=(POOL_GROUPS,),
        in_specs=[pl.BlockSpec((POOL_BUF, b, POOL_GC), lambda g: (0, 0, g)),
                  pl.BlockSpec((b, POOL_GC), lambda g: (0, g)),
                  pl.BlockSpec((1, POOL_GC, POOL_GC), lambda g: (g, 0, 0)),
                  pl.BlockSpec((1, 1, POOL_GC), lambda g: (g, 0, 0))],
        out_specs=pl.BlockSpec((b, POOL_GC), lambda g: (0, g)),
        out_shape=jax.ShapeDtypeStruct((b, POOL_WIDTH), BF16),
        compiler_params=_params("parallel"),
        name="pool_sample",
    )(prev_t, p, w_group.astype(BF16), scale.reshape(POOL_GROUPS, 1, POOL_GC))


def _attn_prompt_kernel(q_ref, kv_ref, o_ref):
    for h in range(MEM_HEADS):
        lo, hi = h * MEM_HEAD_DIM, (h + 1) * MEM_HEAD_DIM
        qh = q_ref[:, lo:hi].astype(BF16)
        kh = kv_ref[:, lo:hi].astype(BF16)
        vh = kv_ref[:, MEM_WIDTH + lo:MEM_WIDTH + hi].astype(BF16)
        s = lax.dot_general(qh, kh, _NT, preferred_element_type=F32) * MEM_SCALE
        e = jnp.exp(s - jnp.max(s, axis=-1, keepdims=True))
        l = jnp.sum(e, axis=-1, keepdims=True)
        o = jnp.dot(e.astype(BF16), vh, preferred_element_type=F32) / l
        o_ref[:, lo:hi] = o.astype(o_ref.dtype)


def attn_prompt(u, q_col, kv, *, tq=512):
    assert q_col % MEM_WIDTH == 0
    nt = SEQ // tq
    return pl.pallas_call(
        _attn_prompt_kernel,
        grid=(BATCH, nt),
        in_specs=[pl.BlockSpec((tq, MEM_WIDTH), lambda b, t: (b * nt + t, q_col // MEM_WIDTH)),
                  pl.BlockSpec((MEM_TOKENS, 2 * MEM_WIDTH), lambda b, t: (b, 0))],
        out_specs=pl.BlockSpec((tq, MEM_WIDTH), lambda b, t: (b * nt + t, 0)),
        out_shape=jax.ShapeDtypeStruct((N_PROMPT, MEM_WIDTH), BF16),
        compiler_params=_params("parallel", "parallel"),
        name="attn_prompt",
    )(u, kv)


def _attn_sample_kernel(q_ref, k_ref, v_ref, o_ref, acc_ref):
    i = pl.program_id(0)
    bb = k_ref.shape[0]
    for n in range(bb):
        tok = pl.ds(i * bb + n, 1)
        for h in range(MEM_HEADS):
            lo, hi = h * MEM_HEAD_DIM, (h + 1) * MEM_HEAD_DIM
            s = jnp.sum(k_ref[n, :, h, :] * q_ref[tok, lo:hi], axis=-1, keepdims=True) * MEM_SCALE
            e = jnp.exp(s - jnp.max(s, axis=0, keepdims=True))
            p = e / jnp.sum(e, axis=0, keepdims=True)
            acc_ref[tok, lo:hi] = jnp.sum(p * v_ref[n, :, h, :], axis=0, keepdims=True)

    @pl.when(i == pl.num_programs(0) - 1)
    def _():
        o_ref[...] = acc_ref[...].astype(o_ref.dtype)


def attn_sample(q, cache_k, cache_v, layer, *, bb=4):
    b = q.shape[0]
    cache_spec = pl.BlockSpec((None, bb, MEM_TOKENS, MEM_HEADS, MEM_HEAD_DIM),
                              lambda i: (layer, i, 0, 0, 0))
    return pl.pallas_call(
        _attn_sample_kernel,
        grid=(b // bb,),
        in_specs=[pl.BlockSpec((b, MEM_WIDTH), lambda i: (0, 0)), cache_spec, cache_spec],
        out_specs=pl.BlockSpec((b, MEM_WIDTH), lambda i: (0, 0)),
        out_shape=jax.ShapeDtypeStruct((b, MEM_WIDTH), BF16),
        scratch_shapes=[pltpu.VMEM((b, MEM_WIDTH), F32)],
        compiler_params=_params("arbitrary"),
        name="attn_sample",
    )(q, cache_k, cache_v)


def _log_decay(z, wg_ref, bg_ref):
    g = jnp.dot(z.astype(BF16), wg_ref[...], preferred_element_type=F32) + bg_ref[...]
    return jax.nn.log_sigmoid(g) / GATE_TAU


def _gla_out(o, og, gn):
    on = o * lax.rsqrt(jnp.mean(o * o, axis=-1, keepdims=True) + EPS) * gn
    return on * (og * jax.nn.sigmoid(og))


def _gla_prompt_kernel(q_ref, k_ref, v_ref, og_ref, z_ref, wg_ref, bg_ref, gn_ref,
                       mix_ref, st_ref, s_ref):
    n = pl.program_id(1)
    c = q_ref.shape[0]

    @pl.when(n == 0)
    def _():
        s_ref[...] = jnp.zeros_like(s_ref)

    la = _log_decay(z_ref[...], wg_ref, bg_ref)
    row = lax.broadcasted_iota(jnp.int32, (c, 1), 0)
    b = la
    k = 1
    while k < c:
        b = b + jnp.where(row >= k, pltpu.roll(b, k, 0), 0.0)
        k *= 2
    b_last = b[c - 1:c, :]
    q_in = q_ref[...] * jnp.exp(b) * GLA_SCALE
    k_in = k_ref[...] * jnp.exp(-b)
    k_out = k_ref[...] * jnp.exp(b_last - b)
    decay = jnp.exp(b_last)
    causal = row >= lax.broadcasted_iota(jnp.int32, (1, c), 1)
    gn = gn_ref[...]
    for h in range(GLA_HEADS):
        klo, khi = h * GLA_DK, (h + 1) * GLA_DK
        vlo, vhi = h * GLA_DV, (h + 1) * GLA_DV
        qh = q_in[:, klo:khi].astype(BF16)
        vh = v_ref[:, vlo:vhi].astype(BF16)
        a = lax.dot_general(qh, k_in[:, klo:khi].astype(BF16), _NT, preferred_element_type=F32)
        a = jnp.where(causal, a, 0.0)
        st = s_ref[h]
        o = (jnp.dot(a.astype(BF16), vh, preferred_element_type=F32)
             + lax.dot_general(qh, st.astype(BF16), _NT, preferred_element_type=F32))
        s_ref[h] = st * decay[:, klo:khi] + lax.dot_general(
            vh, k_out[:, klo:khi].astype(BF16), _TN, preferred_element_type=F32)
        mix_ref[:, vlo:vhi] = _gla_out(o, og_ref[:, vlo:vhi], gn).astype(mix_ref.dtype)

    @pl.when(n == pl.num_programs(1) - 1)
    def _():
        st_ref[0] = s_ref[...]


def gla_prompt(u, wg_pad, b_gate, norm_g):
    nc = SEQ // GLA_CHUNK
    c = GLA_CHUNK

    def col(width, off):
        assert off % width == 0
        return pl.BlockSpec((c, width), lambda b, n: (b * nc + n, off // width))

    return pl.pallas_call(
        _gla_prompt_kernel,
        grid=(BATCH, nc),
        in_specs=[col(GLA_KEY, GLA_COL_Q), col(GLA_KEY, GLA_COL_K), col(GLA_VAL, GLA_COL_V),
                  col(GLA_VAL, GLA_COL_OG), col(LANES, GLA_COL_Z),
                  pl.BlockSpec((LANES, GLA_KEY), lambda b, n: (0, 0)),
                  pl.BlockSpec((1, GLA_KEY), lambda b, n: (0, 0)),
                  pl.BlockSpec((1, GLA_DV), lambda b, n: (0, 0))],
        out_specs=[pl.BlockSpec((c, GLA_VAL), lambda b, n: (b * nc + n, 0)),
                   pl.BlockSpec((1, GLA_HEADS, GLA_DV, GLA_DK), lambda b, n: (b, 0, 0, 0))],
        out_shape=[jax.ShapeDtypeStruct((N_PROMPT, GLA_VAL), BF16),
                   jax.ShapeDtypeStruct((BATCH, GLA_HEADS, GLA_DV, GLA_DK), F32)],
        scratch_shapes=[pltpu.VMEM((GLA_HEADS, GLA_DV, GLA_DK), F32)],
        compiler_params=_params("parallel", "arbitrary"),
        name="gla_prompt",
    )(u, u, u, u, u, wg_pad, b_gate.reshape(1, GLA_KEY), norm_g.reshape(1, GLA_DV))


def _gla_sample_kernel(us_ref, s0_ref, wg_ref, bg_ref, gn_ref, mix_ref, s1_ref, la_ref):
    i = pl.program_id(0)

    @pl.when(i == 0)
    def _():
        la_ref[...] = _log_decay(us_ref[:, GLA_COL_Z:GLA_COL_Z + LANES], wg_ref, bg_ref)

    tok = pl.ds(i, 1)
    gn = gn_ref[...]
    for h in range(GLA_HEADS):
        klo, khi = h * GLA_DK, (h + 1) * GLA_DK
        vlo, vhi = h * GLA_DV, (h + 1) * GLA_DV
        la = la_ref[tok, klo:khi]
        q = us_ref[tok, GLA_COL_Q + klo:GLA_COL_Q + khi]
        k = us_ref[tok, GLA_COL_K + klo:GLA_COL_K + khi]
        v = us_ref[tok, GLA_COL_V + vlo:GLA_COL_V + vhi]
        og = us_ref[tok, GLA_COL_OG + vlo:GLA_COL_OG + vhi]
        q_in = q * jnp.exp(la) * GLA_SCALE
        k_in = k * jnp.exp(-la)
        k_out = k * jnp.exp(la - la)
        decay = jnp.exp(la)
        a = jnp.sum(q_in * k_in, axis=-1, keepdims=True)
        rows = jnp.concatenate(
            [q_in, k_out, decay, jnp.zeros((SUBLANES - 3, GLA_DK), F32)], axis=0)
        cols = jnp.transpose(rows)
        s0 = s0_ref[0, h]
        o = a * v + jnp.sum(cols[:, 0:1] * s0, axis=0, keepdims=True)
        s1_ref[0, h] = s0 * cols[:, 2:3] + cols[:, 1:2] * v
        mix_ref[tok, vlo:vhi] = _gla_out(o, og, gn).astype(mix_ref.dtype)


def gla_sample(us, s0, wg_pad, b_gate, norm_g):
    b = us.shape[0]
    return pl.pallas_call(
        _gla_sample_kernel,
        grid=(b,),
        in_specs=[pl.BlockSpec((b, GLA_IN_COLS), lambda i: (0, 0)),
                  pl.BlockSpec((1, GLA_HEADS, GLA_DK, GLA_DV), lambda i: (i, 0, 0, 0)),
                  pl.BlockSpec((LANES, GLA_KEY), lambda i: (0, 0)),
                  pl.BlockSpec((1, GLA_KEY), lambda i: (0, 0)),
                  pl.BlockSpec((1, GLA_DV), lambda i: (0, 0))],
        out_specs=[pl.BlockSpec((b, GLA_VAL), lambda i: (0, 0)),
                   pl.BlockSpec((1, GLA_HEADS, GLA_DK, GLA_DV), lambda i: (i, 0, 0, 0))],
        out_shape=[jax.ShapeDtypeStruct((b, GLA_VAL), F32),
                   jax.ShapeDtypeStruct(s0.shape, F32)],
        scratch_shapes=[pltpu.VMEM((b, GLA_KEY), F32)],
        compiler_params=_params("arbitrary"),
        name="gla_sample",
    )(us, s0, wg_pad, b_gate.reshape(1, GLA_KEY), norm_g.reshape(1, GLA_DV))


_PEER_CAND = tuple((a, b) for a in range(PEER_TOPK) for b in range(PEER_TOPK)
                   if (a + 1) * (b + 1) <= PEER_TOPK)
_NEG_INF = float("-inf")
_UNRANKED = float(N_KEYS)


_REDUCE_WIDTH = 8


def _reduce(op, vals):
    accs = list(vals[:_REDUCE_WIDTH])
    for n, v in enumerate(vals[_REDUCE_WIDTH:]):
        accs[n % _REDUCE_WIDTH] = op(accs[n % _REDUCE_WIDTH], v)
    while len(accs) > 1:
        accs = [op(accs[n], accs[n + 1]) if n + 1 < len(accs) else accs[n]
                for n in range(0, len(accs), 2)]
    return accs[0]


def _peer_select_kernel(q_ref, keys_ref, e0_ref, li_ref, e1_ref, rb_ref,
                        s_ref, cur_ref, rank_ref, sv_ref, ix_ref, cand_ref, cnt_ref,
                        *, rt, rows_are_heads):
    n_rows = s_ref.shape[1] // N_KEYS
    tile = (n_rows, LANES)

    def key_rows(k):
        return slice(k * n_rows, (k + 1) * n_rows)

    for p in range(2):
        for r in range(n_rows):
            if rows_are_heads:
                col0 = (2 * r + p) * PEER_HALF
                qs, kk = q_ref[:, col0:col0 + PEER_HALF], keys_ref[2 * r + p]
            else:
                qs = q_ref[r * LANES:(r + 1) * LANES, p * PEER_HALF:(p + 1) * PEER_HALF]
                kk = keys_ref[p]
            s_ref[p, pl.ds(r, N_KEYS, stride=n_rows), :] = lax.dot_general(
                kk, qs.astype(BF16), _NT, preferred_element_type=F32)

    for p in range(2):
        top = _reduce(jnp.maximum, [s_ref[p, key_rows(k)] for k in range(N_KEYS)])

        def start(p=p):
            cur_ref[...] = s_ref[p]
            if p == 1:
                rank_ref[...] = jnp.full(rank_ref.shape, _UNRANKED, F32)

        def extract(r, m, p=p, one_key=False):
            sv_ref[p, pl.ds(r, 1)] = m[None]
            rf = lax.convert_element_type(r, F32)
            if one_key:
                idx = _reduce(jnp.minimum,
                              [jnp.where(cur_ref[key_rows(k)] == m, float(k), _UNRANKED)
                               for k in range(N_KEYS)])
            else:
                idx = jnp.full(tile, _UNRANKED, F32)
            rest = []
            for k in range(N_KEYS):
                old = cur_ref[key_rows(k)]
                hit = (idx == float(k)) if one_key else (old == m)
                c = jnp.where(hit, _NEG_INF, old)
                cur_ref[key_rows(k)] = c
                if p == 1:
                    rank_ref[key_rows(k)] = jnp.where(hit, rf, rank_ref[key_rows(k)])
                elif not one_key:
                    idx = jnp.where(hit, float(k), idx)
                rest.append(c)
            if p == 0:
                ix_ref[pl.ds(r, 1)] = idx[None]
            return _reduce(jnp.maximum, rest)

        start()
        lax.fori_loop(0, PEER_TOPK, extract, top)
        removed = _reduce(jnp.add, [jnp.where(cur_ref[key_rows(k)] == _NEG_INF, 1.0, 0.0)
                                    for k in range(N_KEYS)])

        @pl.when(jnp.max(jnp.abs(removed - float(PEER_TOPK))) > 0.0)
        def _(start=start, extract=extract, top=top):
            start()
            lax.fori_loop(0, PEER_TOPK, functools.partial(extract, one_key=True), top)

    for ci, (a, b) in enumerate(_PEER_CAND):
        cand_ref[ci] = sv_ref[0, a] + sv_ref[1, b]
    cnt_ref[...] = jnp.zeros_like(cnt_ref)
    c_max = sv_ref[0, 0] + sv_ref[1, 0]
    no_flat = float(PEER_TOPK * PEER_TOPK)

    def pick(_, z):
        m = _reduce(jnp.maximum, [cand_ref[ci] for ci in range(len(_PEER_CAND))])
        flat = _reduce(jnp.minimum, [
            jnp.where(cand_ref[ci] == m, float(a * PEER_TOPK + b), no_flat)
            for ci, (a, b) in enumerate(_PEER_CAND)])
        for ci, (a, b) in enumerate(_PEER_CAND):
            cand_ref[ci] = jnp.where(flat == float(a * PEER_TOPK + b), _NEG_INF, cand_ref[ci])
        first = jnp.floor(flat * (1.0 / PEER_TOPK))
        for a in range(PEER_TOPK):
            cnt_ref[a] = cnt_ref[a] + jnp.where(first == float(a), 1.0, 0.0)
        return z + jnp.exp(m - c_max)

    inv_z = 1.0 / lax.fori_loop(0, PEER_TOPK, pick, jnp.zeros(tile, F32))

    max0 = sv_ref[0, 0]
    max1 = sv_ref[1, 0]

    for k in range(N_KEYS):
        li = jnp.zeros(tile, F32)
        for a in range(PEER_TOPK):
            li = jnp.where(ix_ref[a] == float(k), cnt_ref[a], li)
        e0 = jnp.exp(s_ref[0, key_rows(k)] - max0) * inv_z
        e1 = jnp.exp(s_ref[1, key_rows(k)] - max1)
        rb = rank_ref[key_rows(k)]
        for b in range(n_rows // rt):
            src = slice(b * rt, (b + 1) * rt)
            dst = slice(k * rt, (k + 1) * rt)
            e0_ref[0, b, dst, :] = e0[src]
            li_ref[0, b, dst, :] = li[src]
            e1_ref[0, b, dst, :] = e1[src]
            rb_ref[0, b, dst, :] = rb[src]


def peer_select(q, keys_bf16, *, rows, rblk, rt):
    assert rows % rblk == 0 and rblk % rt == 0
    rows_are_heads = rblk == 1
    if rows_are_heads:
        n_tile = PEER_HEADS
        grid = (1, rows)
        in_specs = [pl.BlockSpec((LANES, q.shape[1]), lambda h, r: (r, 0)),
                    pl.BlockSpec(keys_bf16.shape, lambda h, r: (0, 0, 0))]
        out = jax.ShapeDtypeStruct((rows, PEER_HEADS, N_KEYS, LANES), F32)
        out_spec = pl.BlockSpec((1, PEER_HEADS, N_KEYS, LANES), lambda h, r: (r, 0, 0, 0))
    else:
        n_tile = rblk
        grid = (PEER_HEADS, rows // rblk)
        in_specs = [pl.BlockSpec((rblk * LANES, 2 * PEER_HALF), lambda h, r: (r, h)),
                    pl.BlockSpec((2, N_KEYS, PEER_HALF), lambda h, r: (h, 0, 0))]
        out = jax.ShapeDtypeStruct((PEER_HEADS, rows // rt, N_KEYS * rt, LANES), F32)
        out_spec = pl.BlockSpec((1, rblk // rt, N_KEYS * rt, LANES), lambda h, r: (h, r, 0, 0))
    tile = (n_tile, LANES)
    res = pl.pallas_call(
        functools.partial(_peer_select_kernel, rt=rt, rows_are_heads=rows_are_heads),
        grid=grid,
        in_specs=in_specs,
        out_specs=[out_spec] * 4,
        out_shape=[out] * 4,
        scratch_shapes=[pltpu.VMEM((2, N_KEYS * n_tile, LANES), F32),
                        pltpu.VMEM((N_KEYS * n_tile, LANES), F32),
                        pltpu.VMEM((N_KEYS * n_tile, LANES), F32),
                        pltpu.VMEM((2, PEER_TOPK) + tile, F32),
                        pltpu.VMEM((PEER_TOPK,) + tile, F32),
                        pltpu.VMEM((len(_PEER_CAND),) + tile, F32),
                        pltpu.VMEM((PEER_TOPK,) + tile, F32)],
        compiler_params=_params("parallel", "parallel"),
        name="peer_select",
    )(q, keys_bf16)
    return [jnp.swapaxes(r, 0, 1) for r in res] if rows_are_heads else res


_DENSE_CHUNK_KEYS = 32
_DENSE_GROUP_KEYS1 = 4
_DENSE_ROWS_PER_DOT = 512


def _gelu(x):
    return 0.5 * x * (1.0 + lax.erf(x * (2.0 ** -0.5)))


def _peer_dense_stages(ht_ref, u_ref, vt_ref, e0_ref, li_ref, e1_ref, rb_ref, o_ref,
                       a_wr, a_rd, w_wr, w_rd, *, ib):
    tn = ht_ref.shape[1]

    for r0 in range(0, ib * N_KEYS, _DENSE_ROWS_PER_DOT):
        rs = slice(r0, r0 + _DENSE_ROWS_PER_DOT)
        a_wr[rs, :] = jnp.dot(u_ref[rs, :], ht_ref[...], preferred_element_type=F32)

    jc = _DENSE_CHUNK_KEYS
    rt = tn // LANES
    for i0 in range(0, ib, _DENSE_GROUP_KEYS1):
        iis = range(i0, i0 + _DENSE_GROUP_KEYS1)
        for j0 in range(0, N_KEYS, jc):
            for c in range(rt):
                lanes = slice(c * LANES, (c + 1) * LANES)
                js = pl.ds(j0 * rt + c, jc, stride=rt)
                g = [None] * len(iis)
                for h in range(PEER_HEADS):
                    rb = rb_ref[h, 0, js, :]
                    e1 = e1_ref[h, 0, js, :]
                    for n, ii in enumerate(iis):
                        first = slice(ii * rt + c, ii * rt + c + 1)
                        e0 = e0_ref[h, 0, first, :]
                        li = li_ref[h, 0, first, :]
                        t = jnp.where(rb < li, e0 * e1, 0.0)
                        g[n] = t if g[n] is None else g[n] + t
                for n, ii in enumerate(iis):
                    rows = slice(ii * N_KEYS + j0, ii * N_KEYS + j0 + jc)
                    w_wr[rows, lanes] = (_gelu(a_rd[rows, lanes]) * g[n]).astype(w_wr.dtype)

    for r0 in range(0, vt_ref.shape[0], 2 * _DENSE_ROWS_PER_DOT):
        rs = slice(r0, r0 + 2 * _DENSE_ROWS_PER_DOT)
        o_ref[rs, :] += jnp.dot(vt_ref[rs, :], w_rd[...], preferred_element_type=F32)


def _peer_dense_kernel(ht_ref, u_ref, vt_ref, e0_ref, li_ref, e1_ref, rb_ref, o_ref,
                       a0_ref, a1_ref, w0_ref, w1_ref, *, ib, n_eb):
    s = pl.program_id(0)

    @pl.when(s == 0)
    def _():
        for ref in (a0_ref, a1_ref, w0_ref, w1_ref):
            ref[...] = jnp.zeros_like(ref)

    @pl.when((s < 2) | ((s - 2) % n_eb == 0))
    def _():
        o_ref[...] = jnp.zeros_like(o_ref)

    io = (ht_ref, u_ref, vt_ref, e0_ref, li_ref, e1_ref, rb_ref, o_ref)

    @pl.when(s % 2 == 0)
    def _():
        _peer_dense_stages(*io, a0_ref, a1_ref, w1_ref, w0_ref, ib=ib)

    @pl.when(s % 2 == 1)
    def _():
        _peer_dense_stages(*io, a1_ref, a0_ref, w0_ref, w1_ref, ib=ib)


def peer_dense(h_t, u_bf16, vt_bf16, e0, li, e1, rb, *, tn, ib=8):
    d, ntok = h_t.shape
    rt = tn // LANES
    n_nb = ntok // tn
    ebs = ib * N_KEYS
    assert e1.shape == (PEER_HEADS, n_nb, N_KEYS * rt, LANES) and ntok == n_nb * tn
    assert N_KEYS % ib == 0 and ib % _DENSE_GROUP_KEYS1 == 0
    n_eb = N_KEYS // ib
    total = n_nb * n_eb

    def stage(lag):
        return lambda s: jnp.clip(s - lag, 0, total - 1)

    fa, fb, fc = stage(0), stage(1), stage(2)
    return pl.pallas_call(
        functools.partial(_peer_dense_kernel, ib=ib, n_eb=n_eb),
        grid=(total + 2,),
        in_specs=[pl.BlockSpec((d, tn), lambda s: (0, fa(s) // n_eb)),
                  pl.BlockSpec((ebs, d), lambda s: (fa(s) % n_eb, 0)),
                  pl.BlockSpec((d, ebs), lambda s: (0, fc(s) % n_eb)),
                  pl.BlockSpec((PEER_HEADS, 1, ib * rt, LANES),
                               lambda s: (0, fb(s) // n_eb, fb(s) % n_eb, 0)),
                  pl.BlockSpec((PEER_HEADS, 1, ib * rt, LANES),
                               lambda s: (0, fb(s) // n_eb, fb(s) % n_eb, 0)),
                  pl.BlockSpec((PEER_HEADS, 1, N_KEYS * rt, LANES),
                               lambda s: (0, fb(s) // n_eb, 0, 0)),
                  pl.BlockSpec((PEER_HEADS, 1, N_KEYS * rt, LANES),
                               lambda s: (0, fb(s) // n_eb, 0, 0))],
        out_specs=pl.BlockSpec((d, tn), lambda s: (0, fc(s) // n_eb)),
        out_shape=jax.ShapeDtypeStruct((d, ntok), F32),
        scratch_shapes=[pltpu.VMEM((ebs, tn), F32), pltpu.VMEM((ebs, tn), F32),
                        pltpu.VMEM((ebs, tn), BF16), pltpu.VMEM((ebs, tn), BF16)],
        compiler_params=_params("arbitrary"),
        name="peer_dense",
    )(h_t, u_bf16, vt_bf16, e0, li, e1, rb)


def _table_kernel(x_ref, o_ref, *, transposed):
    x = x_ref[...]
    o_ref[...] = (jnp.transpose(x) if transposed else x).astype(o_ref.dtype)


def expert_table_bf16(tabs, layer, *, transposed, rows=512):
    _, e, d = tabs.shape
    assert e % rows == 0
    if transposed:
        out_shape, out_spec = (d, e), pl.BlockSpec((d, rows), lambda i: (0, i))
    else:
        out_shape, out_spec = (e, d), pl.BlockSpec((rows, d), lambda i: (i, 0))
    return pl.pallas_call(
        functools.partial(_table_kernel, transposed=transposed),
        grid=(e // rows,),
        in_specs=[pl.BlockSpec((None, rows, d), lambda i: (layer, i, 0))],
        out_specs=out_spec,
        out_shape=jax.ShapeDtypeStruct(out_shape, BF16),
        compiler_params=_params("parallel"),
        name="expert_table",
    )(tabs)


def _add_transposed_kernel(x_ref, yt_ref, o_ref):
    o_ref[...] = x_ref[...] + jnp.transpose(yt_ref[...])


def add_transposed(x, y_t, *, tm):
    n, d = x.shape
    return pl.pallas_call(
        _add_transposed_kernel,
        grid=(n // tm,),
        in_specs=[pl.BlockSpec((tm, d), lambda i: (i, 0)), pl.BlockSpec((d, tm), lambda i: (0, i))],
        out_specs=pl.BlockSpec((tm, d), lambda i: (i, 0)),
        out_shape=jax.ShapeDtypeStruct((n, d), F32),
        compiler_params=_params("parallel"),
        name="add_transposed",
    )(x, y_t)


_TM = (2048, DEC_BATCH)
_TB = (512, DEC_BATCH)


def peer_layer(xs, norm_g, w_q, keys, u_tabs, v_tabs, layer):
    w_q = w_q.astype(BF16)
    keys_bf16 = keys.reshape(2 * PEER_HEADS, N_KEYS, PEER_HALF).astype(BF16)
    u_bf16 = expert_table_bf16(u_tabs, layer, transposed=False)
    vt_bf16 = expert_table_bf16(v_tabs, layer, transposed=True)
    out = []
    for x, tm, tb in zip(xs, _TM, _TB):
        h, h_t = rmsnorm(x, norm_g, BF16, tm=tb, transposed=True)
        q = matmul([h], [w_q], tm=tm, tn=512)
        rows = x.shape[0] // LANES
        sel = peer_select(q, keys_bf16, rows=rows, rblk=min(rows, 32), rt=tb // LANES)
        y_t = peer_dense(h_t, u_bf16, vt_bf16, *sel, tn=tb)
        out.append(add_transposed(x, y_t, tm=tb))
    return out


def kernel(x_prompt, x_sample, cache_mem_k, cache_mem_v, state_pool, state_gla, mem_prompt,
           norm_mix, norm_ffn, norm_final, mem_norm, w_mem_kv,
           pool_w_in, pool_w_group, pool_scale, pool_w_out,
           gla_w_in, gla_w_gate, gla_b_gate, gla_norm, gla_w_out,
           peer_w_q, peer_keys, peer_u, peer_v):
    d = D_MODEL
    xs = [x_prompt.reshape(N_PROMPT, d), x_sample.reshape(DEC_BATCH, d)]

    def project(xs, norm_g, w):
        return [matmul([rmsnorm(x, norm_g, BF16, tm=tb)], [w], tm=tm, tn=512)
                for x, tm, tb in zip(xs, _TM, _TB)]

    def mix_out(xs, mixes, atts, w, split):
        return [matmul([m, a], [w[:split], w[split:]], x, tm=tm // 2, tn=512)
                for x, m, a, tm in zip(xs, mixes, atts, _TM)]

    mem = mem_prompt.reshape(BATCH * MEM_TOKENS, d)
    kvs = []
    for i in range(DEPTH):
        mn = rmsnorm(mem, mem_norm[i], BF16, tm=512)
        kvs.append(matmul([mn], [w_mem_kv[i].astype(BF16)], tm=512, tn=512))

    def split_kv(kv, lo):
        return kv[:, lo:lo + MEM_WIDTH].reshape(BATCH, MEM_TOKENS, MEM_HEADS, MEM_HEAD_DIM)

    mem_k_prompt = jnp.stack([split_kv(kv, 0) for kv in kvs])
    mem_v_prompt = jnp.stack([split_kv(kv, MEM_WIDTH) for kv in kvs])

    u_p, u_s = project(xs, norm_mix[0], pool_w_in[0].astype(BF16))
    p_s = u_s[:, :POOL_WIDTH]
    mixes = [pool_prompt(u_p, pool_w_group[0], pool_scale[0]),
             pool_sample(jnp.transpose(state_pool[0], (1, 0, 2)), p_s, pool_w_group[0], pool_scale[0])]
    atts = [attn_prompt(u_p, POOL_WIDTH, kvs[0]),
            attn_sample(u_s[:, POOL_WIDTH:], cache_mem_k, cache_mem_v, 0)]
    xs = mix_out(xs, mixes, atts, pool_w_out[0].astype(BF16), POOL_WIDTH)
    pool_prompt_state = u_p[:, :POOL_WIDTH].reshape(BATCH, SEQ, POOL_WIDTH)[:, SEQ - POOL_BUF:][None]
    pool_sample_state = jnp.concatenate([state_pool[0][:, 1:], p_s[:, None, :]], axis=1)[None]
    xs = peer_layer(xs, norm_ffn[0], peer_w_q[0], peer_keys[0], peer_u, peer_v, 0)

    w_in = gla_w_in[0]
    o3 = 2 * GLA_KEY + GLA_VAL
    o4 = o3 + GATE_RANK
    o5 = o4 + GLA_VAL
    w_in = jnp.concatenate([
        w_in[:, :o3], w_in[:, o4:o5], w_in[:, o5:], w_in[:, o3:o4],
        jnp.zeros((d, GLA_Z_PAD - GATE_RANK), w_in.dtype)], axis=1).astype(BF16)
    wg_pad = jnp.concatenate(
        [gla_w_gate[0], jnp.zeros((LANES - GATE_RANK, GLA_KEY), gla_w_gate.dtype)], axis=0).astype(BF16)
    u_p, u_s = project(xs, norm_mix[1], w_in)
    mix_p, st_t = gla_prompt(u_p, wg_pad, gla_b_gate[0], gla_norm[0])
    mix_s, gla_s = gla_sample(u_s, state_gla[0], wg_pad, gla_b_gate[0], gla_norm[0])
    atts = [attn_prompt(u_p, GLA_COL_QM, kvs[1]),
            attn_sample(u_s[:, GLA_COL_QM:GLA_COL_QM + MEM_WIDTH], cache_mem_k, cache_mem_v, 1)]
    xs = mix_out(xs, [mix_p, mix_s.astype(BF16)], atts, gla_w_out[0].astype(BF16), GLA_VAL)
    xs = peer_layer(xs, norm_ffn[1], peer_w_q[1], peer_keys[1], peer_u, peer_v, 1)

    y_prompt, y_sample = [rmsnorm(x, norm_final, F32, tm=tb) for x, tb in zip(xs, _TB)]
    return (y_prompt.reshape(BATCH, SEQ, d),
            y_sample.reshape(DEC_BATCH, 1, d),
            pool_prompt_state,
            jnp.swapaxes(st_t, -1, -2)[None],
            mem_k_prompt,
            mem_v_prompt,
            pool_sample_state,
            gla_s[None])
```

```python
import functools

import jax
import jax.numpy as jnp
from jax import lax
from jax.experimental import pallas as pl
from jax.experimental.pallas import tpu as pltpu

D_MODEL = 2048
BATCH = 4
SEQ = 2048
DEPTH = 2
DEC_BATCH = 128
PAST_LEN = 16384
EPS = 1e-6

N_PROMPT = BATCH * SEQ

POOL_WIDTH = D_MODEL // 2
POOL_GROUPS = 4
POOL_GC = POOL_WIDTH // POOL_GROUPS
POOL_WINDOWS = (2, 4, 8, 16)
POOL_BUF = max(POOL_WINDOWS) - 1

MEM_TOKENS = 256
MEM_HEADS = 4
MEM_HEAD_DIM = D_MODEL // 8
MEM_WIDTH = MEM_HEADS * MEM_HEAD_DIM
MEM_SCALE = MEM_HEAD_DIM ** -0.5

GLA_HEADS = 4
GLA_KEY = D_MODEL // 2
GLA_VAL = D_MODEL
GLA_DK = GLA_KEY // GLA_HEADS
GLA_DV = GLA_VAL // GLA_HEADS
GATE_RANK = 16
GATE_TAU = 16.0
GLA_CHUNK = 64
GLA_SCALE = GLA_DK ** -0.5

N_KEYS = 128
N_EXPERTS = N_KEYS * N_KEYS
PEER_HEADS = 8
PEER_HALF = 128
PEER_TOPK = 16

LANES = 128
SUBLANES = 8
VMEM_LIMIT = 56 * 1024 * 1024

GLA_COL_Q = 0
GLA_COL_K = GLA_KEY
GLA_COL_V = 2 * GLA_KEY
GLA_COL_OG = 2 * GLA_KEY + GLA_VAL
GLA_COL_QM = 2 * GLA_KEY + 2 * GLA_VAL
GLA_COL_Z = GLA_COL_QM + MEM_WIDTH
GLA_Z_PAD = 512
GLA_IN_COLS = GLA_COL_Z + GLA_Z_PAD

BF16 = jnp.bfloat16
F32 = jnp.float32

_NT = (((1,), (1,)), ((), ()))
_TN = (((0,), (0,)), ((), ()))


def _params(*sem):
    return pltpu.CompilerParams(dimension_semantics=sem, vmem_limit_bytes=VMEM_LIMIT)


def _rmsnorm_kernel(x_ref, g_ref, *o_refs):
    x = x_ref[...]
    y = x * lax.rsqrt(jnp.mean(x * x, axis=-1, keepdims=True) + EPS) * g_ref[...]
    o_refs[0][...] = y.astype(o_refs[0].dtype)
    if len(o_refs) > 1:
        o_refs[1][...] = jnp.transpose(y).astype(o_refs[1].dtype)


def rmsnorm(x, g, out_dtype, *, tm, transposed=False):
    rows, d = x.shape
    assert rows % tm == 0
    out_shape = [jax.ShapeDtypeStruct((rows, d), out_dtype)]
    out_specs = [pl.BlockSpec((tm, d), lambda i: (i, 0))]
    if transposed:
        out_shape.append(jax.ShapeDtypeStruct((d, rows), out_dtype))
        out_specs.append(pl.BlockSpec((d, tm), lambda i: (0, i)))
    res = pl.pallas_call(
        _rmsnorm_kernel,
        grid=(rows // tm,),
        in_specs=[pl.BlockSpec((tm, d), lambda i: (i, 0)),
                  pl.BlockSpec((1, d), lambda i: (0, 0))],
        out_specs=out_specs,
        out_shape=out_shape,
        compiler_params=_params("parallel"),
        name="rmsnorm",
    )(x, g.reshape(1, d))
    return res if transposed else res[0]


def _matmul_kernel(*refs, n_pairs, has_res):
    o_ref = refs[-1]
    acc = None
    for a_ref, w_ref in zip(refs[:n_pairs], refs[n_pairs:2 * n_pairs]):
        p = jnp.dot(a_ref[...], w_ref[...], preferred_element_type=F32)
        acc = p if acc is None else acc + p
    if has_res:
        acc = acc + refs[2 * n_pairs][...]
    o_ref[...] = acc.astype(o_ref.dtype)


def matmul(a_list, w_list, residual=None, *, tm, tn):
    n = a_list[0].shape[0]
    f = w_list[0].shape[1]
    assert n % tm == 0 and f % tn == 0
    in_specs = [pl.BlockSpec((tm, a.shape[1]), lambda i, j: (i, 0)) for a in a_list]
    in_specs += [pl.BlockSpec((w.shape[0], tn), lambda i, j: (0, j)) for w in w_list]
    args = list(a_list) + list(w_list)
    if residual is not None:
        in_specs.append(pl.BlockSpec((tm, tn), lambda i, j: (i, j)))
        args.append(residual)
    return pl.pallas_call(
        functools.partial(_matmul_kernel, n_pairs=len(a_list), has_res=residual is not None),
        grid=(n // tm, f // tn),
        in_specs=in_specs,
        out_specs=pl.BlockSpec((tm, tn), lambda i, j: (i, j)),
        out_shape=jax.ShapeDtypeStruct((n, f), F32),
        compiler_params=_params("parallel", "parallel"),
        name="matmul",
    )(*args)


def _pool_prompt_kernel(p_ref, wg_ref, sc_ref, o_ref):
    g = pl.program_id(1)
    x = p_ref[...]
    t_len = x.shape[0]
    row = lax.broadcasted_iota(jnp.int32, (t_len, 1), 0)

    def shifted(v, k):
        return jnp.where(row >= k, pltpu.roll(v, k, 0), 0.0)

    for gi, w in enumerate(POOL_WINDOWS):
        @pl.when(g == gi)
        def _(w=w):
            s = x
            k = 1
            while k < w:
                s = s + shifted(s, k)
                k *= 2
            cnt = jnp.minimum(w, row + 1).astype(F32)
            d = s / cnt - x
            out = jnp.dot(d.astype(BF16), wg_ref[0], preferred_element_type=F32)
            o_ref[...] = (out * sc_ref[0]).astype(o_ref.dtype)


def pool_prompt(u, w_group, scale):
    return pl.pallas_call(
        _pool_prompt_kernel,
        grid=(BATCH, POOL_GROUPS),
        in_specs=[pl.BlockSpec((SEQ, POOL_GC), lambda b, g: (b, g)),
                  pl.BlockSpec((1, POOL_GC, POOL_GC), lambda b, g: (g, 0, 0)),
                  pl.BlockSpec((1, 1, POOL_GC), lambda b, g: (g, 0, 0))],
        out_specs=pl.BlockSpec((SEQ, POOL_GC), lambda b, g: (b, g)),
        out_shape=jax.ShapeDtypeStruct((N_PROMPT, POOL_WIDTH), BF16),
        compiler_params=_params("parallel", "parallel"),
        name="pool_prompt",
    )(u, w_group.astype(BF16), scale.reshape(POOL_GROUPS, 1, POOL_GC))


def _pool_sample_kernel(prev_ref, p_ref, wg_ref, sc_ref, o_ref):
    g = pl.program_id(0)
    x = p_ref[...]
    for gi, w in enumerate(POOL_WINDOWS):
        @pl.when(g == gi)
        def _(w=w):
            s = x
            for r in range(POOL_BUF - (w - 1), POOL_BUF):
                s = s + prev_ref[r]
            cnt = float(min(w, PAST_LEN + 1))
            d = s / cnt - x
            out = jnp.dot(d.astype(BF16), wg_ref[0], preferred_element_type=F32)
            o_ref[...] = (out * sc_ref[0]).astype(o_ref.dtype)


def pool_sample(prev_t, p, w_group, scale):
    b = p.shape[0]
    return pl.pallas_call(
        _pool_sample_kernel,
        grid=(POOL_GROUPS,),
        in_specs=[pl.BlockSpec((POOL_BUF, b, POOL_GC), lambda g: (0, 0, g)),
                  pl.BlockSpec((b, POOL_GC), lambda g: (0, g)),
                  pl.BlockSpec((1, POOL_GC, POOL_GC), lambda g: (g, 0, 0)),
                  pl.BlockSpec((1, 1, POOL_GC), lambda g: (g, 0, 0))],
        out_specs=pl.BlockSpec((b, POOL_GC), lambda g: (0, g)),
        out_shape=jax.ShapeDtypeStruct((b, POOL_WIDTH), BF16),
        compiler_params=_params("parallel"),
        name="pool_sample",
    )(prev_t, p, w_group.astype(BF16), scale.reshape(POOL_GROUPS, 1, POOL_GC))


def _attn_prompt_kernel(q_ref, kv_ref, o_ref):
    for h in range(MEM_HEADS):
        lo, hi = h * MEM_HEAD_DIM, (h + 1) * MEM_HEAD_DIM
        qh = q_ref[:, lo:hi].astype(BF16)
        kh = kv_ref[:, lo:hi].astype(BF16)
        vh = kv_ref[:, MEM_WIDTH + lo:MEM_WIDTH + hi].astype(BF16)
        s = lax.dot_general(qh, kh, _NT, preferred_element_type=F32) * MEM_SCALE
        e = jnp.exp(s - jnp.max(s, axis=-1, keepdims=True))
        l = jnp.sum(e, axis=-1, keepdims=True)
        o = jnp.dot(e.astype(BF16), vh, preferred_element_type=F32) / l
        o_ref[:, lo:hi] = o.astype(o_ref.dtype)


def attn_prompt(u, q_col, kv, *, tq=512):
    assert q_col % MEM_WIDTH == 0
    nt = SEQ // tq
    return pl.pallas_call(
        _attn_prompt_kernel,
        grid=(BATCH, nt),
        in_specs=[pl.BlockSpec((tq, MEM_WIDTH), lambda b, t: (b * nt + t, q_col // MEM_WIDTH)),
                  pl.BlockSpec((MEM_TOKENS, 2 * MEM_WIDTH), lambda b, t: (b, 0))],
        out_specs=pl.BlockSpec((tq, MEM_WIDTH), lambda b, t: (b * nt + t, 0)),
        out_shape=jax.ShapeDtypeStruct((N_PROMPT, MEM_WIDTH), BF16),
        compiler_params=_params("parallel", "parallel"),
        name="attn_prompt",
    )(u, kv)


def _attn_sample_kernel(q_ref, k_ref, v_ref, o_ref, acc_ref):
    i = pl.program_id(0)
    bb = k_ref.shape[0]
    for n in range(bb):
        row = i * bb + n
        s = jnp.sum(k_ref[n] * q_ref[row][None], axis=-1, keepdims=True) * MEM_SCALE
        e = jnp.exp(s - jnp.max(s, axis=0, keepdims=True))
        p = e / jnp.sum(e, axis=0, keepdims=True)
        acc_ref[row] = jnp.sum(p * v_ref[n], axis=0)

    @pl.when(i == pl.num_programs(0) - 1)
    def _():
        o_ref[...] = acc_ref[...].astype(o_ref.dtype)


def attn_sample(q, cache_k, cache_v, layer, *, bb=4):
    b = q.shape[0]
    heads = (b, MEM_HEADS, MEM_HEAD_DIM)
    cache_spec = pl.BlockSpec((None, bb, MEM_TOKENS, MEM_HEADS, MEM_HEAD_DIM),
                              lambda i: (layer, i, 0, 0, 0))
    return pl.pallas_call(
        _attn_sample_kernel,
        grid=(b // bb,),
        in_specs=[pl.BlockSpec(heads, lambda i: (0, 0, 0)), cache_spec, cache_spec],
        out_specs=pl.BlockSpec(heads, lambda i: (0, 0, 0)),
        out_shape=jax.ShapeDtypeStruct(heads, BF16),
        scratch_shapes=[pltpu.VMEM(heads, F32)],
        compiler_params=_params("arbitrary"),
        name="attn_sample",
    )(q.reshape(heads), cache_k, cache_v).reshape(b, MEM_WIDTH)


def _log_decay(z, wg_ref, bg_ref):
    g = jnp.dot(z.astype(BF16), wg_ref[...], preferred_element_type=F32) + bg_ref[...]
    return jax.nn.log_sigmoid(g) / GATE_TAU


def _gla_out(o, og, gn):
    on = o * lax.rsqrt(jnp.mean(o * o, axis=-1, keepdims=True) + EPS) * gn
    return on * (og * jax.nn.sigmoid(og))


def _gla_prompt_kernel(q_ref, k_ref, v_ref, og_ref, z_ref, wg_ref, bg_ref, gn_ref,
                       mix_ref, st_ref, s_ref):
    n = pl.program_id(1)
    c = q_ref.shape[0]

    @pl.when(n == 0)
    def _():
        s_ref[...] = jnp.zeros_like(s_ref)

    la = _log_decay(z_ref[...], wg_ref, bg_ref)
    row = lax.broadcasted_iota(jnp.int32, (c, 1), 0)
    b = la
    k = 1
    while k < c:
        b = b + jnp.where(row >= k, pltpu.roll(b, k, 0), 0.0)
        k *= 2
    b_last = b[c - 1:c, :]
    q_in = q_ref[...] * jnp.exp(b) * GLA_SCALE
    k_in = k_ref[...] * jnp.exp(-b)
    k_out = k_ref[...] * jnp.exp(b_last - b)
    decay = jnp.exp(b_last)
    causal = row >= lax.broadcasted_iota(jnp.int32, (1, c), 1)
    gn = gn_ref[...]
    for h in range(GLA_HEADS):
        klo, khi = h * GLA_DK, (h + 1) * GLA_DK
        vlo, vhi = h * GLA_DV, (h + 1) * GLA_DV
        qh = q_in[:, klo:khi].astype(BF16)
        vh = v_ref[:, vlo:vhi].astype(BF16)
        a = lax.dot_general(qh, k_in[:, klo:khi].astype(BF16), _NT, preferred_element_type=F32)
        a = jnp.where(causal, a, 0.0)
        st = s_ref[h]
        o = (jnp.dot(a.astype(BF16), vh, preferred_element_type=F32)
             + lax.dot_general(qh, st.astype(BF16), _NT, preferred_element_type=F32))
        s_ref[h] = st * decay[:, klo:khi] + lax.dot_general(
            vh, k_out[:, klo:khi].astype(BF16), _TN, preferred_element_type=F32)
        mix_ref[:, vlo:vhi] = _gla_out(o, og_ref[:, vlo:vhi], gn).astype(mix_ref.dtype)

    @pl.when(n == pl.num_programs(1) - 1)
    def _():
        st_ref[0] = s_ref[...]


def gla_prompt(u, wg_pad, b_gate, norm_g):
    nc = SEQ // GLA_CHUNK
    c = GLA_CHUNK

    def col(width, off):
        assert off % width == 0
        return pl.BlockSpec((c, width), lambda b, n: (b * nc + n, off // width))

    return pl.pallas_call(
        _gla_prompt_kernel,
        grid=(BATCH, nc),
        in_specs=[col(GLA_KEY, GLA_COL_Q), col(GLA_KEY, GLA_COL_K), col(GLA_VAL, GLA_COL_V),
                  col(GLA_VAL, GLA_COL_OG), col(LANES, GLA_COL_Z),
                  pl.BlockSpec((LANES, GLA_KEY), lambda b, n: (0, 0)),
                  pl.BlockSpec((1, GLA_KEY), lambda b, n: (0, 0)),
                  pl.BlockSpec((1, GLA_DV), lambda b, n: (0, 0))],
        out_specs=[pl.BlockSpec((c, GLA_VAL), lambda b, n: (b * nc + n, 0)),
                   pl.BlockSpec((1, GLA_HEADS, GLA_DV, GLA_DK), lambda b, n: (b, 0, 0, 0))],
        out_shape=[jax.ShapeDtypeStruct((N_PROMPT, GLA_VAL), BF16),
                   jax.ShapeDtypeStruct((BATCH, GLA_HEADS, GLA_DV, GLA_DK), F32)],
        scratch_shapes=[pltpu.VMEM((GLA_HEADS, GLA_DV, GLA_DK), F32)],
        compiler_params=_params("parallel", "arbitrary"),
        name="gla_prompt",
    )(u, u, u, u, u, wg_pad, b_gate.reshape(1, GLA_KEY), norm_g.reshape(1, GLA_DV))


def _gla_sample_kernel(us_ref, s0_ref, wg_ref, bg_ref, gn_ref, mix_ref, s1_ref, la_ref):
    i = pl.program_id(0)

    @pl.when(i == 0)
    def _():
        la_ref[...] = _log_decay(us_ref[:, GLA_COL_Z:GLA_COL_Z + LANES], wg_ref, bg_ref)

    tok = pl.ds(i, 1)
    gn = gn_ref[...]
    for h in range(GLA_HEADS):
        klo, khi = h * GLA_DK, (h + 1) * GLA_DK
        vlo, vhi = h * GLA_DV, (h + 1) * GLA_DV
        la = la_ref[tok, klo:khi]
        q = us_ref[tok, GLA_COL_Q + klo:GLA_COL_Q + khi]
        k = us_ref[tok, GLA_COL_K + klo:GLA_COL_K + khi]
        v = us_ref[tok, GLA_COL_V + vlo:GLA_COL_V + vhi]
        og = us_ref[tok, GLA_COL_OG + vlo:GLA_COL_OG + vhi]
        q_in = q * jnp.exp(la) * GLA_SCALE
        k_in = k * jnp.exp(-la)
        k_out = k * jnp.exp(la - la)
        decay = jnp.exp(la)
        a = jnp.sum(q_in * k_in, axis=-1, keepdims=True)
        rows = jnp.concatenate(
            [q_in, k_out, decay, jnp.zeros((SUBLANES - 3, GLA_DK), F32)], axis=0)
        cols = jnp.transpose(rows)
        s0 = s0_ref[0, h]
        o = a * v + jnp.sum(cols[:, 0:1] * s0, axis=0, keepdims=True)
        s1_ref[0, h] = s0 * cols[:, 2:3] + cols[:, 1:2] * v
        mix_ref[tok, vlo:vhi] = _gla_out(o, og, gn).astype(mix_ref.dtype)


def gla_sample(us, s0, wg_pad, b_gate, norm_g):
    b = us.shape[0]
    return pl.pallas_call(
        _gla_sample_kernel,
        grid=(b,),
        in_specs=[pl.BlockSpec((b, GLA_IN_COLS), lambda i: (0, 0)),
                  pl.BlockSpec((1, GLA_HEADS, GLA_DK, GLA_DV), lambda i: (i, 0, 0, 0)),
                  pl.BlockSpec((LANES, GLA_KEY), lambda i: (0, 0)),
                  pl.BlockSpec((1, GLA_KEY), lambda i: (0, 0)),
                  pl.BlockSpec((1, GLA_DV), lambda i: (0, 0))],
        out_specs=[pl.BlockSpec((b, GLA_VAL), lambda i: (0, 0)),
                   pl.BlockSpec((1, GLA_HEADS, GLA_DK, GLA_DV), lambda i: (i, 0, 0, 0))],
        out_shape=[jax.ShapeDtypeStruct((b, GLA_VAL), F32),
                   jax.ShapeDtypeStruct(s0.shape, F32)],
        scratch_shapes=[pltpu.VMEM((b, GLA_KEY), F32)],
        compiler_params=_params("arbitrary"),
        name="gla_sample",
    )(us, s0, wg_pad, b_gate.reshape(1, GLA_KEY), norm_g.reshape(1, GLA_DV))


_PEER_CAND = tuple((a, b) for a in range(PEER_TOPK) for b in range(PEER_TOPK)
                   if (a + 1) * (b + 1) <= PEER_TOPK)
_NEG_INF = float("-inf")
_UNRANKED = float(N_KEYS)


_REDUCE_WIDTH = 8


def _reduce(op, vals):
    accs = list(vals[:_REDUCE_WIDTH])
    for n, v in enumerate(vals[_REDUCE_WIDTH:]):
        accs[n % _REDUCE_WIDTH] = op(accs[n % _REDUCE_WIDTH], v)
    while len(accs) > 1:
        accs = [op(accs[n], accs[n + 1]) if n + 1 < len(accs) else accs[n]
                for n in range(0, len(accs), 2)]
    return accs[0]


def _peer_select_kernel(q_ref, keys_ref, e0_ref, li_ref, e1_ref, rb_ref,
                        s_ref, cur_ref, rank_ref, sv_ref, ix_ref, cand_ref, cnt_ref,
                        *, rt, rows_are_heads):
    n_rows = s_ref.shape[1] // N_KEYS
    tile = (n_rows, LANES)

    def key_rows(k):
        return slice(k * n_rows, (k + 1) * n_rows)

    for p in range(2):
        for r in range(n_rows):
            if rows_are_heads:
                col0 = (2 * r + p) * PEER_HALF
                qs, kk = q_ref[:, col0:col0 + PEER_HALF], keys_ref[2 * r + p]
            else:
                qs = q_ref[r * LANES:(r + 1) * LANES, p * PEER_HALF:(p + 1) * PEER_HALF]
                kk = keys_ref[p]
            s_ref[p, pl.ds(r, N_KEYS, stride=n_rows), :] = lax.dot_general(
                kk, qs.astype(BF16), _NT, preferred_element_type=F32)

    for p in range(2):
        top = _reduce(jnp.maximum, [s_ref[p, key_rows(k)] for k in range(N_KEYS)])

        def start(p=p):
            cur_ref[...] = s_ref[p]
            if p == 1:
                rank_ref[...] = jnp.full(rank_ref.shape, _UNRANKED, F32)

        def extract(r, m, p=p, one_key=False):
            sv_ref[p, pl.ds(r, 1)] = m[None]
            rf = lax.convert_element_type(r, F32)
            if one_key:
                idx = _reduce(jnp.minimum,
                              [jnp.where(cur_ref[key_rows(k)] == m, float(k), _UNRANKED)
                               for k in range(N_KEYS)])
            else:
                idx = jnp.full(tile, _UNRANKED, F32)
            rest = []
            for k in range(N_KEYS):
                old = cur_ref[key_rows(k)]
                hit = (idx == float(k)) if one_key else (old == m)
                c = jnp.where(hit, _NEG_INF, old)
                cur_ref[key_rows(k)] = c
                if p == 1:
                    rank_ref[key_rows(k)] = jnp.where(hit, rf, rank_ref[key_rows(k)])
                elif not one_key:
                    idx = jnp.where(hit, float(k), idx)
                rest.append(c)
            if p == 0:
                ix_ref[pl.ds(r, 1)] = idx[None]
            return _reduce(jnp.maximum, rest)

        start()
        lax.fori_loop(0, PEER_TOPK, extract, top)
        removed = _reduce(jnp.add, [jnp.where(cur_ref[key_rows(k)] == _NEG_INF, 1.0, 0.0)
                                    for k in range(N_KEYS)])

        @pl.when(jnp.max(jnp.abs(removed - float(PEER_TOPK))) > 0.0)
        def _(start=start, extract=extract, top=top):
            start()
            lax.fori_loop(0, PEER_TOPK, functools.partial(extract, one_key=True), top)

    for ci, (a, b) in enumerate(_PEER_CAND):
        cand_ref[ci] = sv_ref[0, a] + sv_ref[1, b]
    cnt_ref[...] = jnp.zeros_like(cnt_ref)
    c_max = sv_ref[0, 0] + sv_ref[1, 0]
    no_flat = float(PEER_TOPK * PEER_TOPK)

    def pick(_, z):
        m = _reduce(jnp.maximum, [cand_ref[ci] for ci in range(len(_PEER_CAND))])
        flat = _reduce(jnp.minimum, [
            jnp.where(cand_ref[ci] == m, float(a * PEER_TOPK + b), no_flat)
            for ci, (a, b) in enumerate(_PEER_CAND)])
        for ci, (a, b) in enumerate(_PEER_CAND):
            cand_ref[ci] = jnp.where(flat == float(a * PEER_TOPK + b), _NEG_INF, cand_ref[ci])
        first = jnp.floor(flat * (1.0 / PEER_TOPK))
        for a in range(PEER_TOPK):
            cnt_ref[a] = cnt_ref[a] + jnp.where(first == float(a), 1.0, 0.0)
        return z + jnp.exp(m - c_max)

    inv_z = 1.0 / lax.fori_loop(0, PEER_TOPK, pick, jnp.zeros(tile, F32))

    max0 = sv_ref[0, 0]
    max1 = sv_ref[1, 0]

    for k in range(N_KEYS):
        li = jnp.zeros(tile, F32)
        for a in range(PEER_TOPK):
            li = jnp.where(ix_ref[a] == float(k), cnt_ref[a], li)
        e0 = jnp.exp(s_ref[0, key_rows(k)] - max0) * inv_z
        e1 = jnp.exp(s_ref[1, key_rows(k)] - max1)
        rb = rank_ref[key_rows(k)]
        for b in range(n_rows // rt):
            src = slice(b * rt, (b + 1) * rt)
            dst = slice(k * rt, (k + 1) * rt)
            e0_ref[0, b, dst, :] = e0[src]
            li_ref[0, b, dst, :] = li[src]
            e1_ref[0, b, dst, :] = e1[src]
            rb_ref[0, b, dst, :] = rb[src]


def peer_select(q, keys_bf16, *, rows, rblk, rt):
    assert rows % rblk == 0 and rblk % rt == 0
    rows_are_heads = rblk == 1
    if rows_are_heads:
        n_tile = PEER_HEADS
        grid = (1, rows)
        in_specs = [pl.BlockSpec((LANES, q.shape[1]), lambda h, r: (r, 0)),
                    pl.BlockSpec(keys_bf16.shape, lambda h, r: (0, 0, 0))]
        out = jax.ShapeDtypeStruct((rows, PEER_HEADS, N_KEYS, LANES), F32)
        out_spec = pl.BlockSpec((1, PEER_HEADS, N_KEYS, LANES), lambda h, r: (r, 0, 0, 0))
    else:
        n_tile = rblk
        grid = (PEER_HEADS, rows // rblk)
        in_specs = [pl.BlockSpec((rblk * LANES, 2 * PEER_HALF), lambda h, r: (r, h)),
                    pl.BlockSpec((2, N_KEYS, PEER_HALF), lambda h, r: (h, 0, 0))]
        out = jax.ShapeDtypeStruct((PEER_HEADS, rows // rt, N_KEYS * rt, LANES), F32)
        out_spec = pl.BlockSpec((1, rblk // rt, N_KEYS * rt, LANES), lambda h, r: (h, r, 0, 0))
    tile = (n_tile, LANES)
    res = pl.pallas_call(
        functools.partial(_peer_select_kernel, rt=rt, rows_are_heads=rows_are_heads),
        grid=grid,
        in_specs=in_specs,
        out_specs=[out_spec] * 4,
        out_shape=[out] * 4,
        scratch_shapes=[pltpu.VMEM((2, N_KEYS * n_tile, LANES), F32),
                        pltpu.VMEM((N_KEYS * n_tile, LANES), F32),
                        pltpu.VMEM((N_KEYS * n_tile, LANES), F32),
                        pltpu.VMEM((2, PEER_TOPK) + tile, F32),
                        pltpu.VMEM((PEER_TOPK,) + tile, F32),
                        pltpu.VMEM((len(_PEER_CAND),) + tile, F32),
                        pltpu.VMEM((PEER_TOPK,) + tile, F32)],
        compiler_params=_params("parallel", "parallel"),
        name="peer_select",
    )(q, keys_bf16)
    return [jnp.swapaxes(r, 0, 1) for r in res] if rows_are_heads else res


_DENSE_CHUNK_KEYS = 32
_DENSE_GROUP_KEYS1 = 4
_DENSE_ROWS_PER_DOT = 512


def _gelu(x):
    return 0.5 * x * (1.0 + lax.erf(x * (2.0 ** -0.5)))


def _peer_dense_stages(ht_ref, u_ref, vt_ref, e0_ref, li_ref, e1_ref, rb_ref, o_ref,
                       a_wr, a_rd, w_wr, w_rd, *, ib):
    tn = ht_ref.shape[1]

    for r0 in range(0, ib * N_KEYS, _DENSE_ROWS_PER_DOT):
        rs = slice(r0, r0 + _DENSE_ROWS_PER_DOT)
        a_wr[rs, :] = jnp.dot(u_ref[rs, :], ht_ref[...], preferred_element_type=F32)

    jc = _DENSE_CHUNK_KEYS
    rt = tn // LANES
    for i0 in range(0, ib, _DENSE_GROUP_KEYS1):
        iis = range(i0, i0 + _DENSE_GROUP_KEYS1)
        for j0 in range(0, N_KEYS, jc):
            for c in range(rt):
                lanes = slice(c * LANES, (c + 1) * LANES)
                js = pl.ds(j0 * rt + c, jc, stride=rt)
                g = [None] * len(iis)
                for h in range(PEER_HEADS):
                    rb = rb_ref[h, 0, js, :]
                    e1 = e1_ref[h, 0, js, :]
                    for n, ii in enumerate(iis):
                        first = slice(ii * rt + c, ii * rt + c + 1)
                        e0 = e0_ref[h, 0, first, :]
                        li = li_ref[h, 0, first, :]
                        t = jnp.where(rb < li, e0 * e1, 0.0)
                        g[n] = t if g[n] is None else g[n] + t
                for n, ii in enumerate(iis):
                    rows = slice(ii * N_KEYS + j0, ii * N_KEYS + j0 + jc)
                    w_wr[rows, lanes] = (_gelu(a_rd[rows, lanes]) * g[n]).astype(w_wr.dtype)

    for r0 in range(0, vt_ref.shape[0], 2 * _DENSE_ROWS_PER_DOT):
        rs = slice(r0, r0 + 2 * _DENSE_ROWS_PER_DOT)
        o_ref[rs, :] += jnp.dot(vt_ref[rs, :], w_rd[...], preferred_element_type=F32)


def _peer_dense_kernel(ht_ref, u_ref, vt_ref, e0_ref, li_ref, e1_ref, rb_ref, o_ref,
                       a0_ref, a1_ref, w0_ref, w1_ref, *, ib, n_eb):
    s = pl.program_id(0)

    @pl.when(s == 0)
    def _():
        for ref in (a0_ref, a1_ref, w0_ref, w1_ref):
            ref[...] = jnp.zeros_like(ref)

    @pl.when((s < 2) | ((s - 2) % n_eb == 0))
    def _():
        o_ref[...] = jnp.zeros_like(o_ref)

    io = (ht_ref, u_ref, vt_ref, e0_ref, li_ref, e1_ref, rb_ref, o_ref)

    @pl.when(s % 2 == 0)
    def _():
        _peer_dense_stages(*io, a0_ref, a1_ref, w1_ref, w0_ref, ib=ib)

    @pl.when(s % 2 == 1)
    def _():
        _peer_dense_stages(*io, a1_ref, a0_ref, w0_ref, w1_ref, ib=ib)


def peer_dense(h_t, u_bf16, vt_bf16, e0, li, e1, rb, *, tn, ib=8):
    d, ntok = h_t.shape
    rt = tn // LANES
    n_nb = ntok // tn
    ebs = ib * N_KEYS
    assert e1.shape == (PEER_HEADS, n_nb, N_KEYS * rt, LANES) and ntok == n_nb * tn
    assert N_KEYS % ib == 0 and ib % _DENSE_GROUP_KEYS1 == 0
    n_eb = N_KEYS // ib
    total = n_nb * n_eb

    def stage(lag):
        return lambda s: jnp.clip(s - lag, 0, total - 1)

    fa, fb, fc = stage(0), stage(1), stage(2)
    return pl.pallas_call(
        functools.partial(_peer_dense_kernel, ib=ib, n_eb=n_eb),
        grid=(total + 2,),
        in_specs=[pl.BlockSpec((d, tn), lambda s: (0, fa(s) // n_eb)),
                  pl.BlockSpec((ebs, d), lambda s: (fa(s) % n_eb, 0)),
                  pl.BlockSpec((d, ebs), lambda s: (0, fc(s) % n_eb)),
                  pl.BlockSpec((PEER_HEADS, 1, ib * rt, LANES),
                               lambda s: (0, fb(s) // n_eb, fb(s) % n_eb, 0)),
                  pl.BlockSpec((PEER_HEADS, 1, ib * rt, LANES),
                               lambda s: (0, fb(s) // n_eb, fb(s) % n_eb, 0)),
                  pl.BlockSpec((PEER_HEADS, 1, N_KEYS * rt, LANES),
                               lambda s: (0, fb(s) // n_eb, 0, 0)),
                  pl.BlockSpec((PEER_HEADS, 1, N_KEYS * rt, LANES),
                               lambda s: (0, fb(s) // n_eb, 0, 0))],
        out_specs=pl.BlockSpec((d, tn), lambda s: (0, fc(s) // n_eb)),
        out_shape=jax.ShapeDtypeStruct((d, ntok), F32),
        scratch_shapes=[pltpu.VMEM((ebs, tn), F32), pltpu.VMEM((ebs, tn), F32),
                        pltpu.VMEM((ebs, tn), BF16), pltpu.VMEM((ebs, tn), BF16)],
        compiler_params=_params("arbitrary"),
        name="peer_dense",
    )(h_t, u_bf16, vt_bf16, e0, li, e1, rb)


def _table_kernel(x_ref, o_ref, *, transposed):
    x = x_ref[...]
    o_ref[...] = (jnp.transpose(x) if transposed else x).astype(o_ref.dtype)


def expert_table_bf16(tabs, layer, *, transposed, rows=512):
    _, e, d = tabs.shape
    assert e % rows == 0
    if transposed:
        out_shape, out_spec = (d, e), pl.BlockSpec((d, rows), lambda i: (0, i))
    else:
        out_shape, out_spec = (e, d), pl.BlockSpec((rows, d), lambda i: (i, 0))
    return pl.pallas_call(
        functools.partial(_table_kernel, transposed=transposed),
        grid=(e // rows,),
        in_specs=[pl.BlockSpec((None, rows, d), lambda i: (layer, i, 0))],
        out_specs=out_spec,
        out_shape=jax.ShapeDtypeStruct(out_shape, BF16),
        compiler_params=_params("parallel"),
        name="expert_table",
    )(tabs)


def _add_transposed_kernel(x_ref, yt_ref, o_ref):
    o_ref[...] = x_ref[...] + jnp.transpose(yt_ref[...])


def add_transposed(x, y_t, *, tm):
    n, d = x.shape
    return pl.pallas_call(
        _add_transposed_kernel,
        grid=(n // tm,),
        in_specs=[pl.BlockSpec((tm, d), lambda i: (i, 0)), pl.BlockSpec((d, tm), lambda i: (0, i))],
        out_specs=pl.BlockSpec((tm, d), lambda i: (i, 0)),
        out_shape=jax.ShapeDtypeStruct((n, d), F32),
        compiler_params=_params("parallel"),
        name="add_transposed",
    )(x, y_t)


_TM = (2048, DEC_BATCH)
_TB = (512, DEC_BATCH)


def peer_layer(xs, norm_g, w_q, keys, u_tabs, v_tabs, layer):
    w_q = w_q.astype(BF16)
    keys_bf16 = keys.reshape(2 * PEER_HEADS, N_KEYS, PEER_HALF).astype(BF16)
    u_bf16 = expert_table_bf16(u_tabs, layer, transposed=False)
    vt_bf16 = expert_table_bf16(v_tabs, layer, transposed=True)
    out = []
    for x, tm, tb in zip(xs, _TM, _TB):
        h, h_t = rmsnorm(x, norm_g, BF16, tm=tb, transposed=True)
        q = matmul([h], [w_q], tm=tm, tn=512)
        rows = x.shape[0] // LANES
        sel = peer_select(q, keys_bf16, rows=rows, rblk=min(rows, 32), rt=tb // LANES)
        y_t = peer_dense(h_t, u_bf16, vt_bf16, *sel, tn=tb)
        out.append(add_transposed(x, y_t, tm=tb))
    return out


def kernel(x_prompt, x_sample, cache_mem_k, cache_mem_v, state_pool, state_gla, mem_prompt,
           norm_mix, norm_ffn, norm_final, mem_norm, w_mem_kv,
           pool_w_in, pool_w_group, pool_scale, pool_w_out,
           gla_w_in, gla_w_gate, gla_b_gate, gla_norm, gla_w_out,
           peer_w_q, peer_keys, peer_u, peer_v):
    d = D_MODEL
    xs = [x_prompt.reshape(N_PROMPT, d), x_sample.reshape(DEC_BATCH, d)]

    def project(xs, norm_g, w):
        return [matmul([rmsnorm(x, norm_g, BF16, tm=tb)], [w], tm=tm, tn=512)
                for x, tm, tb in zip(xs, _TM, _TB)]

    def mix_out(xs, mixes, atts, w, split):
        return [matmul([m, a], [w[:split], w[split:]], x, tm=tm // 2, tn=512)
                for x, m, a, tm in zip(xs, mixes, atts, _TM)]

    mem = mem_prompt.reshape(BATCH * MEM_TOKENS, d)
    kvs = []
    for i in range(DEPTH):
        mn = rmsnorm(mem, mem_norm[i], BF16, tm=512)
        kvs.append(matmul([mn], [w_mem_kv[i].astype(BF16)], tm=512, tn=512))

    def split_kv(kv, lo):
        return kv[:, lo:lo + MEM_WIDTH].reshape(BATCH, MEM_TOKENS, MEM_HEADS, MEM_HEAD_DIM)

    mem_k_prompt = jnp.stack([split_kv(kv, 0) for kv in kvs])
    mem_v_prompt = jnp.stack([split_kv(kv, MEM_WIDTH) for kv in kvs])

    u_p, u_s = project(xs, norm_mix[0], pool_w_in[0].astype(BF16))
    p_s = u_s[:, :POOL_WIDTH]
    mixes = [pool_prompt(u_p, pool_w_group[0], pool_scale[0]),
             pool_sample(jnp.transpose(state_pool[0], (1, 0, 2)), p_s, pool_w_group[0], pool_scale[0])]
    atts = [attn_prompt(u_p, POOL_WIDTH, kvs[0]),
            attn_sample(u_s[:, POOL_WIDTH:], cache_mem_k, cache_mem_v, 0)]
    xs = mix_out(xs, mixes, atts, pool_w_out[0].astype(BF16), POOL_WIDTH)
    pool_prompt_state = u_p[:, :POOL_WIDTH].reshape(BATCH, SEQ, POOL_WIDTH)[:, SEQ - POOL_BUF:][None]
    pool_sample_state = jnp.concatenate([state_pool[0][:, 1:], p_s[:, None, :]], axis=1)[None]
    xs = peer_layer(xs, norm_ffn[0], peer_w_q[0], peer_keys[0], peer_u, peer_v, 0)

    w_in = gla_w_in[0]
    o3 = 2 * GLA_KEY + GLA_VAL
    o4 = o3 + GATE_RANK
    o5 = o4 + GLA_VAL
    w_in = jnp.concatenate([
        w_in[:, :o3], w_in[:, o4:o5], w_in[:, o5:], w_in[:, o3:o4],
        jnp.zeros((d, GLA_Z_PAD - GATE_RANK), w_in.dtype)], axis=1).astype(BF16)
    wg_pad = jnp.concatenate(
        [gla_w_gate[0], jnp.zeros((LANES - GATE_RANK, GLA_KEY), gla_w_gate.dtype)], axis=0).astype(BF16)
    u_p, u_s = project(xs, norm_mix[1], w_in)
    mix_p, st_t = gla_prompt(u_p, wg_pad, gla_b_gate[0], gla_norm[0])
    mix_s, gla_s = gla_sample(u_s, state_gla[0], wg_pad, gla_b_gate[0], gla_norm[0])
    atts = [attn_prompt(u_p, GLA_COL_QM, kvs[1]),
            attn_sample(u_s[:, GLA_COL_QM:GLA_COL_QM + MEM_WIDTH], cache_mem_k, cache_mem_v, 1)]
    xs = mix_out(xs, [mix_p, mix_s.astype(BF16)], atts, gla_w_out[0].astype(BF16), GLA_VAL)
    xs = peer_layer(xs, norm_ffn[1], peer_w_q[1], peer_keys[1], peer_u, peer_v, 1)

    y_prompt, y_sample = [rmsnorm(x, norm_final, F32, tm=tb) for x, tb in zip(xs, _TB)]
    return (y_prompt.reshape(BATCH, SEQ, d),
            y_sample.reshape(DEC_BATCH, 1, d),
            pool_prompt_state,
            jnp.swapaxes(st_t, -1, -2)[None],
            mem_k_prompt,
            mem_v_prompt,
            pool_sample_state,
            gla_s[None])
```

```python
import functools

import jax
import jax.numpy as jnp
from jax import lax
from jax.experimental import pallas as pl
from jax.experimental.pallas import tpu as pltpu

D_MODEL = 2048
BATCH = 4
SEQ = 2048
DEPTH = 2
DEC_BATCH = 128
PAST_LEN = 16384
EPS = 1e-6

N_PROMPT = BATCH * SEQ

POOL_WIDTH = D_MODEL // 2
POOL_GROUPS = 4
POOL_GC = POOL_WIDTH // POOL_GROUPS
POOL_WINDOWS = (2, 4, 8, 16)
POOL_BUF = max(POOL_WINDOWS) - 1

MEM_TOKENS = 256
MEM_HEADS = 4
MEM_HEAD_DIM = D_MODEL // 8
MEM_WIDTH = MEM_HEADS * MEM_HEAD_DIM
MEM_SCALE = MEM_HEAD_DIM ** -0.5

GLA_HEADS = 4
GLA_KEY = D_MODEL // 2
GLA_VAL = D_MODEL
GLA_DK = GLA_KEY // GLA_HEADS
GLA_DV = GLA_VAL // GLA_HEADS
GATE_RANK = 16
GATE_TAU = 16.0
GLA_CHUNK = 64
GLA_SCALE = GLA_DK ** -0.5

N_KEYS = 128
N_EXPERTS = N_KEYS * N_KEYS
PEER_HEADS = 8
PEER_HALF = 128
PEER_TOPK = 16

LANES = 128
SUBLANES = 8
VMEM_LIMIT = 56 * 1024 * 1024

GLA_COL_Q = 0
GLA_COL_K = GLA_KEY
GLA_COL_V = 2 * GLA_KEY
GLA_COL_OG = 2 * GLA_KEY + GLA_VAL
GLA_COL_QM = 2 * GLA_KEY + 2 * GLA_VAL
GLA_COL_Z = GLA_COL_QM + MEM_WIDTH
GLA_Z_PAD = 512
GLA_IN_COLS = GLA_COL_Z + GLA_Z_PAD

BF16 = jnp.bfloat16
F32 = jnp.float32

_NT = (((1,), (1,)), ((), ()))
_TN = (((0,), (0,)), ((), ()))


def _params(*sem):
    return pltpu.CompilerParams(dimension_semantics=sem, vmem_limit_bytes=VMEM_LIMIT)


def _rmsnorm_kernel(x_ref, g_ref, *o_refs):
    x = x_ref[...]
    y = x * lax.rsqrt(jnp.mean(x * x, axis=-1, keepdims=True) + EPS) * g_ref[...]
    o_refs[0][...] = y.astype(o_refs[0].dtype)
    if len(o_refs) > 1:
        o_refs[1][...] = jnp.transpose(y).astype(o_refs[1].dtype)


def rmsnorm(x, g, out_dtype, *, tm, transposed=False):
    rows, d = x.shape
    assert rows % tm == 0
    out_shape = [jax.ShapeDtypeStruct((rows, d), out_dtype)]
    out_specs = [pl.BlockSpec((tm, d), lambda i: (i, 0))]
    if transposed:
        out_shape.append(jax.ShapeDtypeStruct((d, rows), out_dtype))
        out_specs.append(pl.BlockSpec((d, tm), lambda i: (0, i)))
    res = pl.pallas_call(
        _rmsnorm_kernel,
        grid=(rows // tm,),
        in_specs=[pl.BlockSpec((tm, d), lambda i: (i, 0)),
                  pl.BlockSpec((1, d), lambda i: (0, 0))],
        out_specs=out_specs,
        out_shape=out_shape,
        compiler_params=_params("parallel"),
        name="rmsnorm",
    )(x, g.reshape(1, d))
    return res if transposed else res[0]


def _matmul_kernel(*refs, n_pairs, has_res):
    o_ref = refs[-1]
    acc = None
    for a_ref, w_ref in zip(refs[:n_pairs], refs[n_pairs:2 * n_pairs]):
        p = jnp.dot(a_ref[...], w_ref[...], preferred_element_type=F32)
        acc = p if acc is None else acc + p
    if has_res:
        acc = acc + refs[2 * n_pairs][...]
    o_ref[...] = acc.astype(o_ref.dtype)


def matmul(a_list, w_list, residual=None, *, tm, tn):
    n = a_list[0].shape[0]
    f = w_list[0].shape[1]
    assert n % tm == 0 and f % tn == 0
    in_specs = [pl.BlockSpec((tm, a.shape[1]), lambda i, j: (i, 0)) for a in a_list]
    in_specs += [pl.BlockSpec((w.shape[0], tn), lambda i, j: (0, j)) for w in w_list]
    args = list(a_list) + list(w_list)
    if residual is not None:
        in_specs.append(pl.BlockSpec((tm, tn), lambda i, j: (i, j)))
        args.append(residual)
    return pl.pallas_call(
        functools.partial(_matmul_kernel, n_pairs=len(a_list), has_res=residual is not None),
        grid=(n // tm, f // tn),
        in_specs=in_specs,
        out_specs=pl.BlockSpec((tm, tn), lambda i, j: (i, j)),
        out_shape=jax.ShapeDtypeStruct((n, f), F32),
        compiler_params=_params("parallel", "parallel"),
        name="matmul",
    )(*args)


def _pool_prompt_kernel(p_ref, wg_ref, sc_ref, o_ref):
    g = pl.program_id(1)
    x = p_ref[...]
    t_len = x.shape[0]
    row = lax.broadcasted_iota(jnp.int32, (t_len, 1), 0)

    def shifted(v, k):
        return jnp.where(row >= k, pltpu.roll(v, k, 0), 0.0)

    for gi, w in enumerate(POOL_WINDOWS):
        @pl.when(g == gi)
        def _(w=w):
            s = x
            k = 1
            while k < w:
                s = s + shifted(s, k)
                k *= 2
            cnt = jnp.minimum(w, row + 1).astype(F32)
            d = s / cnt - x
            out = jnp.dot(d.astype(BF16), wg_ref[0], preferred_element_type=F32)
            o_ref[...] = (out * sc_ref[0]).astype(o_ref.dtype)


def pool_prompt(u, w_group, scale):
    return pl.pallas_call(
        _pool_prompt_kernel,
        grid=(BATCH, POOL_GROUPS),
        in_specs=[pl.BlockSpec((SEQ, POOL_GC), lambda b, g: (b, g)),
                  pl.BlockSpec((1, POOL_GC, POOL_GC), lambda b, g: (g, 0, 0)),
                  pl.BlockSpec((1, 1, POOL_GC), lambda b, g: (g, 0, 0))],
        out_specs=pl.BlockSpec((SEQ, POOL_GC), lambda b, g: (b, g)),
        out_shape=jax.ShapeDtypeStruct((N_PROMPT, POOL_WIDTH), BF16),
        compiler_params=_params("parallel", "parallel"),
        name="pool_prompt",
    )(u, w_group.astype(BF16), scale.reshape(POOL_GROUPS, 1, POOL_GC))


def _pool_sample_kernel(prev_ref, p_ref, wg_ref, sc_ref, o_ref):
    g = pl.program_id(0)
    x = p_ref[...]
    for gi, w in enumerate(POOL_WINDOWS):
        @pl.when(g == gi)
        def _(w=w):
            s = x
            for r in range(POOL_BUF - (w - 1), POOL_BUF):
                s = s + prev_ref[r]
            cnt = float(min(w, PAST_LEN + 1))
            d = s / cnt - x
            out = jnp.dot(d.astype(BF16), wg_ref[0], preferred_element_type=F32)
            o_ref[...] = (out * sc_ref[0]).astype(o_ref.dtype)


def pool_sample(prev_t, p, w_group, scale):
    b = p.shape[0]
    return pl.pallas_call(
        _pool_sample_kernel,
        grid=(POOL_GROUPS,),
        in_specs=[pl.BlockSpec((POOL_BUF, b, POOL_GC), lambda g: (0, 0, g)),
                  pl.BlockSpec((b, POOL_GC), lambda g: (0, g)),
                  pl.BlockSpec((1, POOL_GC, POOL_GC), lambda g: (g, 0, 0)),
                  pl.BlockSpec((1, 1, POOL_GC), lambda g: (g, 0, 0))],
        out_specs=pl.BlockSpec((b, POOL_GC), lambda g: (0, g)),
        out_shape=jax.ShapeDtypeStruct((b, POOL_WIDTH), BF16),
        compiler_params=_params("parallel"),
        name="pool_sample",
    )(prev_t, p, w_group.astype(BF16), scale.reshape(POOL_GROUPS, 1, POOL_GC))


def _attn_prompt_kernel(q_ref, kv_ref, o_ref):
    for h in range(MEM_HEADS):
        lo, hi = h * MEM_HEAD_DIM, (h + 1) * MEM_HEAD_DIM
        qh = q_ref[:, lo:hi].astype(BF16)
        kh = kv_ref[:, lo:hi].astype(BF16)
        vh = kv_ref[:, MEM_WIDTH + lo:MEM_WIDTH + hi].astype(BF16)
        s = lax.dot_general(qh, kh, _NT, preferred_element_type=F32) * MEM_SCALE
        e = jnp.exp(s - jnp.max(s, axis=-1, keepdims=True))
        l = jnp.sum(e, axis=-1, keepdims=True)
        o = jnp.dot(e.astype(BF16), vh, preferred_element_type=F32) / l
        o_ref[:, lo:hi] = o.astype(o_ref.dtype)


def attn_prompt(u, q_col, kv, *, tq=512):
    assert q_col % MEM_WIDTH == 0
    nt = SEQ // tq
    return pl.pallas_call(
        _attn_prompt_kernel,
        grid=(BATCH, nt),
        in_specs=[pl.BlockSpec((tq, MEM_WIDTH), lambda b, t: (b * nt + t, q_col // MEM_WIDTH)),
                  pl.BlockSpec((MEM_TOKENS, 2 * MEM_WIDTH), lambda b, t: (b, 0))],
        out_specs=pl.BlockSpec((tq, MEM_WIDTH), lambda b, t: (b * nt + t, 0)),
        out_shape=jax.ShapeDtypeStruct((N_PROMPT, MEM_WIDTH), BF16),
        compiler_params=_params("parallel", "parallel"),
        name="attn_prompt",
    )(u, kv)


def _attn_sample_kernel(q_ref, k_ref, v_ref, o_ref, acc_ref):
    i = pl.program_id(0)
    bb = k_ref.shape[0]
    for n in range(bb):
        row = i * bb + n
        s = jnp.sum(k_ref[n] * q_ref[row][None], axis=-1, keepdims=True) * MEM_SCALE
        e = jnp.exp(s - jnp.max(s, axis=0, keepdims=True))
        p = e / jnp.sum(e, axis=0, keepdims=True)
        acc_ref[row] = jnp.sum(p * v_ref[n], axis=0)

    @pl.when(i == pl.num_programs(0) - 1)
    def _():
        o_ref[...] = acc_ref[...].astype(o_ref.dtype)


def attn_sample(q, cache_k, cache_v, layer, *, bb=4):
    b = q.shape[0]
    heads = (b, MEM_HEADS, MEM_HEAD_DIM)
    cache_spec = pl.BlockSpec((None, bb, MEM_TOKENS, MEM_HEADS, MEM_HEAD_DIM),
                              lambda i: (layer, i, 0, 0, 0))
    return pl.pallas_call(
        _attn_sample_kernel,
        grid=(b // bb,),
        in_specs=[pl.BlockSpec(heads, lambda i: (0, 0, 0)), cache_spec, cache_spec],
        out_specs=pl.BlockSpec(heads, lambda i: (0, 0, 0)),
        out_shape=jax.ShapeDtypeStruct(heads, BF16),
        scratch_shapes=[pltpu.VMEM(heads, F32)],
        compiler_params=_params("arbitrary"),
        name="attn_sample",
    )(q.reshape(heads), cache_k, cache_v).reshape(b, MEM_WIDTH)


def _log_decay(z, wg_ref, bg_ref):
    g = jnp.dot(z.astype(BF16), wg_ref[...], preferred_element_type=F32) + bg_ref[...]
    return jax.nn.log_sigmoid(g) / GATE_TAU


def _gla_out(o, og, gn):
    on = o * lax.rsqrt(jnp.mean(o * o, axis=-1, keepdims=True) + EPS) * gn
    return on * (og * jax.nn.sigmoid(og))


def _gla_prompt_kernel(q_ref, k_ref, v_ref, og_ref, z_ref, wg_ref, bg_ref, gn_ref,
                       mix_ref, st_ref, s_ref):
    n = pl.program_id(1)
    c = q_ref.shape[0]

    @pl.when(n == 0)
    def _():
        s_ref[...] = jnp.zeros_like(s_ref)

    la = _log_decay(z_ref[...], wg_ref, bg_ref)
    row = lax.broadcasted_iota(jnp.int32, (c, 1), 0)
    b = la
    k = 1
    while k < c:
        b = b + jnp.where(row >= k, pltpu.roll(b, k, 0), 0.0)
        k *= 2
    b_last = b[c - 1:c, :]
    q_in = q_ref[...] * jnp.exp(b) * GLA_SCALE
    k_in = k_ref[...] * jnp.exp(-b)
    k_out = k_ref[...] * jnp.exp(b_last - b)
    decay = jnp.exp(b_last)
    causal = row >= lax.broadcasted_iota(jnp.int32, (1, c), 1)
    gn = gn_ref[...]
    for h in range(GLA_HEADS):
        klo, khi = h * GLA_DK, (h + 1) * GLA_DK
        vlo, vhi = h * GLA_DV, (h + 1) * GLA_DV
        qh = q_in[:, klo:khi].astype(BF16)
        vh = v_ref[:, vlo:vhi].astype(BF16)
        a = lax.dot_general(qh, k_in[:, klo:khi].astype(BF16), _NT, preferred_element_type=F32)
        a = jnp.where(causal, a, 0.0)
        st = s_ref[h]
        o = (jnp.dot(a.astype(BF16), vh, preferred_element_type=F32)
             + lax.dot_general(qh, st.astype(BF16), _NT, preferred_element_type=F32))
        s_ref[h] = st * decay[:, klo:khi] + lax.dot_general(
            vh, k_out[:, klo:khi].astype(BF16), _TN, preferred_element_type=F32)
        mix_ref[:, vlo:vhi] = _gla_out(o, og_ref[:, vlo:vhi], gn).astype(mix_ref.dtype)

    @pl.when(n == pl.num_programs(1) - 1)
    def _():
        st_ref[0] = s_ref[...]


def gla_prompt(u, wg_pad, b_gate, norm_g):
    nc = SEQ // GLA_CHUNK
    c = GLA_CHUNK

    def col(width, off):
        assert off % width == 0
        return pl.BlockSpec((c, width), lambda b, n: (b * nc + n, off // width))

    return pl.pallas_call(
        _gla_prompt_kernel,
        grid=(BATCH, nc),
        in_specs=[col(GLA_KEY, GLA_COL_Q), col(GLA_KEY, GLA_COL_K), col(GLA_VAL, GLA_COL_V),
                  col(GLA_VAL, GLA_COL_OG), col(LANES, GLA_COL_Z),
                  pl.BlockSpec((LANES, GLA_KEY), lambda b, n: (0, 0)),
                  pl.BlockSpec((1, GLA_KEY), lambda b, n: (0, 0)),
                  pl.BlockSpec((1, GLA_DV), lambda b, n: (0, 0))],
        out_specs=[pl.BlockSpec((c, GLA_VAL), lambda b, n: (b * nc + n, 0)),
                   pl.BlockSpec((1, GLA_HEADS, GLA_DV, GLA_DK), lambda b, n: (b, 0, 0, 0))],
        out_shape=[jax.ShapeDtypeStruct((N_PROMPT, GLA_VAL), BF16),
                   jax.ShapeDtypeStruct((BATCH, GLA_HEADS, GLA_DV, GLA_DK), F32)],
        scratch_shapes=[pltpu.VMEM((GLA_HEADS, GLA_DV, GLA_DK), F32)],
        compiler_params=_params("parallel", "arbitrary"),
        name="gla_prompt",
    )(u, u, u, u, u, wg_pad, b_gate.reshape(1, GLA_KEY), norm_g.reshape(1, GLA_DV))


_GLA_SAMPLE_SEQS = 2


def _gla_sample_kernel(us_ref, s0_ref, wg_ref, bg_ref, gn_ref, mix_ref, s1_ref, la_ref):
    i = pl.program_id(0)

    @pl.when(i == 0)
    def _():
        la_ref[...] = _log_decay(us_ref[:, GLA_COL_Z:GLA_COL_Z + LANES], wg_ref, bg_ref)

    gn = gn_ref[...]
    n_seq = s0_ref.shape[0]
    for n, h in [(n, h) for n in range(n_seq) for h in range(GLA_HEADS)]:
        tok = pl.ds(i * n_seq + n, 1)
        klo, khi = h * GLA_DK, (h + 1) * GLA_DK
        vlo, vhi = h * GLA_DV, (h + 1) * GLA_DV
        la = la_ref[tok, klo:khi]
        q = us_ref[tok, GLA_COL_Q + klo:GLA_COL_Q + khi]
        k = us_ref[tok, GLA_COL_K + klo:GLA_COL_K + khi]
        v = us_ref[tok, GLA_COL_V + vlo:GLA_COL_V + vhi]
        og = us_ref[tok, GLA_COL_OG + vlo:GLA_COL_OG + vhi]
        q_in = q * jnp.exp(la) * GLA_SCALE
        k_in = k * jnp.exp(-la)
        k_out = k * jnp.exp(la - la)
        decay = jnp.exp(la)
        a = jnp.sum(q_in * k_in, axis=-1, keepdims=True)
        rows = jnp.concatenate(
            [q_in, k_out, decay, jnp.zeros((SUBLANES - 3, GLA_DK), F32)], axis=0)
        cols = jnp.transpose(rows)
        s0 = s0_ref[n, h]
        o = a * v + jnp.sum(cols[:, 0:1] * s0, axis=0, keepdims=True)
        s1_ref[n, h] = s0 * cols[:, 2:3] + cols[:, 1:2] * v
        mix_ref[tok, vlo:vhi] = _gla_out(o, og, gn).astype(mix_ref.dtype)


def gla_sample(us, s0, wg_pad, b_gate, norm_g):
    b = us.shape[0]
    assert b % _GLA_SAMPLE_SEQS == 0
    state_spec = pl.BlockSpec((_GLA_SAMPLE_SEQS, GLA_HEADS, GLA_DK, GLA_DV), lambda i: (i, 0, 0, 0))
    return pl.pallas_call(
        _gla_sample_kernel,
        grid=(b // _GLA_SAMPLE_SEQS,),
        in_specs=[pl.BlockSpec((b, GLA_IN_COLS), lambda i: (0, 0)),
                  state_spec,
                  pl.BlockSpec((LANES, GLA_KEY), lambda i: (0, 0)),
                  pl.BlockSpec((1, GLA_KEY), lambda i: (0, 0)),
                  pl.BlockSpec((1, GLA_DV), lambda i: (0, 0))],
        out_specs=[pl.BlockSpec((b, GLA_VAL), lambda i: (0, 0)), state_spec],
        out_shape=[jax.ShapeDtypeStruct((b, GLA_VAL), F32),
                   jax.ShapeDtypeStruct(s0.shape, F32)],
        scratch_shapes=[pltpu.VMEM((b, GLA_KEY), F32)],
        compiler_params=_params("arbitrary"),
        name="gla_sample",
    )(us, s0, wg_pad, b_gate.reshape(1, GLA_KEY), norm_g.reshape(1, GLA_DV))


_PEER_CAND = tuple((a, b) for a in range(PEER_TOPK) for b in range(PEER_TOPK)
                   if (a + 1) * (b + 1) <= PEER_TOPK)
_NEG_INF = float("-inf")
_UNRANKED = float(N_KEYS)


_REDUCE_WIDTH = 8


def _reduce(op, vals):
    accs = list(vals[:_REDUCE_WIDTH])
    for n, v in enumerate(vals[_REDUCE_WIDTH:]):
        accs[n % _REDUCE_WIDTH] = op(accs[n % _REDUCE_WIDTH], v)
    while len(accs) > 1:
        accs = [op(accs[n], accs[n + 1]) if n + 1 < len(accs) else accs[n]
                for n in range(0, len(accs), 2)]
    return accs[0]


def _peer_select_kernel(q_ref, keys_ref, e0_ref, li_ref, e1_ref, rb_ref,
                        s_ref, cur_ref, rank_ref, sv_ref, ix_ref, cand_ref, cnt_ref,
                        *, rt, rows_are_heads):
    n_rows = s_ref.shape[1] // N_KEYS
    tile = (n_rows, LANES)

    def key_rows(k):
        return slice(k * n_rows, (k + 1) * n_rows)

    for p in range(2):
        for r in range(n_rows):
            if rows_are_heads:
                col0 = (2 * r + p) * PEER_HALF
                qs, kk = q_ref[:, col0:col0 + PEER_HALF], keys_ref[2 * r + p]
            else:
                qs = q_ref[r * LANES:(r + 1) * LANES, p * PEER_HALF:(p + 1) * PEER_HALF]
                kk = keys_ref[p]
            s_ref[p, pl.ds(r, N_KEYS, stride=n_rows), :] = lax.dot_general(
                kk, qs.astype(BF16), _NT, preferred_element_type=F32)

    for p in range(2):
        top = _reduce(jnp.maximum, [s_ref[p, key_rows(k)] for k in range(N_KEYS)])

        def start(p=p):
            cur_ref[...] = s_ref[p]
            if p == 1:
                rank_ref[...] = jnp.full(rank_ref.shape, _UNRANKED, F32)

        def extract(r, m, p=p, one_key=False):
            sv_ref[p, pl.ds(r, 1)] = m[None]
            rf = lax.convert_element_type(r, F32)
            if one_key:
                idx = _reduce(jnp.minimum,
                              [jnp.where(cur_ref[key_rows(k)] == m, float(k), _UNRANKED)
                               for k in range(N_KEYS)])
            else:
                idx = jnp.full(tile, _UNRANKED, F32)
            rest = []
            for k in range(N_KEYS):
                old = cur_ref[key_rows(k)]
                hit = (idx == float(k)) if one_key else (old == m)
                c = jnp.where(hit, _NEG_INF, old)
                cur_ref[key_rows(k)] = c
                if p == 1:
                    rank_ref[key_rows(k)] = jnp.where(hit, rf, rank_ref[key_rows(k)])
                elif not one_key:
                    idx = jnp.where(hit, float(k), idx)
                rest.append(c)
            if p == 0:
                ix_ref[pl.ds(r, 1)] = idx[None]
            return _reduce(jnp.maximum, rest)

        start()
        lax.fori_loop(0, PEER_TOPK, extract, top)
        removed = _reduce(jnp.add, [jnp.where(cur_ref[key_rows(k)] == _NEG_INF, 1.0, 0.0)
                                    for k in range(N_KEYS)])

        @pl.when(jnp.max(jnp.abs(removed - float(PEER_TOPK))) > 0.0)
        def _(start=start, extract=extract, top=top):
            start()
            lax.fori_loop(0, PEER_TOPK, functools.partial(extract, one_key=True), top)

    for ci, (a, b) in enumerate(_PEER_CAND):
        cand_ref[ci] = sv_ref[0, a] + sv_ref[1, b]
    cnt_ref[...] = jnp.zeros_like(cnt_ref)
    c_max = sv_ref[0, 0] + sv_ref[1, 0]
    no_flat = float(PEER_TOPK * PEER_TOPK)

    def pick(_, z):
        m = _reduce(jnp.maximum, [cand_ref[ci] for ci in range(len(_PEER_CAND))])
        flat = _reduce(jnp.minimum, [
            jnp.where(cand_ref[ci] == m, float(a * PEER_TOPK + b), no_flat)
            for ci, (a, b) in enumerate(_PEER_CAND)])
        for ci, (a, b) in enumerate(_PEER_CAND):
            cand_ref[ci] = jnp.where(flat == float(a * PEER_TOPK + b), _NEG_INF, cand_ref[ci])
        first = jnp.floor(flat * (1.0 / PEER_TOPK))
        for a in range(PEER_TOPK):
            cnt_ref[a] = cnt_ref[a] + jnp.where(first == float(a), 1.0, 0.0)
        return z + jnp.exp(m - c_max)

    inv_z = 1.0 / lax.fori_loop(0, PEER_TOPK, pick, jnp.zeros(tile, F32))

    max0 = sv_ref[0, 0]
    max1 = sv_ref[1, 0]

    for k in range(N_KEYS):
        li = jnp.zeros(tile, F32)
        for a in range(PEER_TOPK):
            li = jnp.where(ix_ref[a] == float(k), cnt_ref[a], li)
        e0 = jnp.exp(s_ref[0, key_rows(k)] - max0) * inv_z
        e1 = jnp.exp(s_ref[1, key_rows(k)] - max1)
        rb = rank_ref[key_rows(k)]
        for b in range(n_rows // rt):
            src = slice(b * rt, (b + 1) * rt)
            dst = slice(k * rt, (k + 1) * rt)
            e0_ref[0, b, dst, :] = e0[src]
            li_ref[0, b, dst, :] = li[src]
            e1_ref[0, b, dst, :] = e1[src]
            rb_ref[0, b, dst, :] = rb[src]


def peer_select(q, keys_bf16, *, rows, rblk, rt):
    assert rows % rblk == 0 and rblk % rt == 0
    rows_are_heads = rblk == 1
    if rows_are_heads:
        n_tile = PEER_HEADS
        grid = (1, rows)
        in_specs = [pl.BlockSpec((LANES, q.shape[1]), lambda h, r: (r, 0)),
                    pl.BlockSpec(keys_bf16.shape, lambda h, r: (0, 0, 0))]
        out = jax.ShapeDtypeStruct((rows, PEER_HEADS, N_KEYS, LANES), F32)
        out_spec = pl.BlockSpec((1, PEER_HEADS, N_KEYS, LANES), lambda h, r: (r, 0, 0, 0))
    else:
        n_tile = rblk
        grid = (PEER_HEADS, rows // rblk)
        in_specs = [pl.BlockSpec((rblk * LANES, 2 * PEER_HALF), lambda h, r: (r, h)),
                    pl.BlockSpec((2, N_KEYS, PEER_HALF), lambda h, r: (h, 0, 0))]
        out = jax.ShapeDtypeStruct((PEER_HEADS, rows // rt, N_KEYS * rt, LANES), F32)
        out_spec = pl.BlockSpec((1, rblk // rt, N_KEYS * rt, LANES), lambda h, r: (h, r, 0, 0))
    tile = (n_tile, LANES)
    res = pl.pallas_call(
        functools.partial(_peer_select_kernel, rt=rt, rows_are_heads=rows_are_heads),
        grid=grid,
        in_specs=in_specs,
        out_specs=[out_spec] * 4,
        out_shape=[out] * 4,
        scratch_shapes=[pltpu.VMEM((2, N_KEYS * n_tile, LANES), F32),
                        pltpu.VMEM((N_KEYS * n_tile, LANES), F32),
                        pltpu.VMEM((N_KEYS * n_tile, LANES), F32),
                        pltpu.VMEM((2, PEER_TOPK) + tile, F32),
                        pltpu.VMEM((PEER_TOPK,) + tile, F32),
                        pltpu.VMEM((len(_PEER_CAND),) + tile, F32),
                        pltpu.VMEM((PEER_TOPK,) + tile, F32)],
        compiler_params=_params("parallel", "parallel"),
        name="peer_select",
    )(q, keys_bf16)
    return [jnp.swapaxes(r, 0, 1) for r in res] if rows_are_heads else res


_DENSE_CHUNK_KEYS = 32
_DENSE_GROUP_KEYS1 = 4
_DENSE_ROWS_PER_DOT = 512


def _gelu(x):
    return 0.5 * x * (1.0 + lax.erf(x * (2.0 ** -0.5)))


def _peer_dense_stages(ht_ref, u_ref, vt_ref, e0_ref, li_ref, e1_ref, rb_ref, o_ref,
                       a_wr, a_rd, w_wr, w_rd, *, ib):
    tn = ht_ref.shape[1]

    for r0 in range(0, ib * N_KEYS, _DENSE_ROWS_PER_DOT):
        rs = slice(r0, r0 + _DENSE_ROWS_PER_DOT)
        a_wr[rs, :] = jnp.dot(u_ref[rs, :], ht_ref[...], preferred_element_type=F32)

    jc = _DENSE_CHUNK_KEYS
    rt = tn // LANES
    for i0 in range(0, ib, _DENSE_GROUP_KEYS1):
        iis = range(i0, i0 + _DENSE_GROUP_KEYS1)
        for j0 in range(0, N_KEYS, jc):
            for c in range(rt):
                lanes = slice(c * LANES, (c + 1) * LANES)
                js = pl.ds(j0 * rt + c, jc, stride=rt)
                g = [None] * len(iis)
                for h in range(PEER_HEADS):
                    rb = rb_ref[h, 0, js, :]
                    e1 = e1_ref[h, 0, js, :]
                    for n, ii in enumerate(iis):
                        first = slice(ii * rt + c, ii * rt + c + 1)
                        e0 = e0_ref[h, 0, first, :]
                        li = li_ref[h, 0, first, :]
                        t = jnp.where(rb < li, e0 * e1, 0.0)
                        g[n] = t if g[n] is None else g[n] + t
                for n, ii in enumerate(iis):
                    rows = slice(ii * N_KEYS + j0, ii * N_KEYS + j0 + jc)
                    w_wr[rows, lanes] = (_gelu(a_rd[rows, lanes]) * g[n]).astype(w_wr.dtype)

    for r0 in range(0, vt_ref.shape[0], 2 * _DENSE_ROWS_PER_DOT):
        rs = slice(r0, r0 + 2 * _DENSE_ROWS_PER_DOT)
        o_ref[rs, :] += jnp.dot(vt_ref[rs, :], w_rd[...], preferred_element_type=F32)


def _peer_dense_kernel(ht_ref, u_ref, vt_ref, e0_ref, li_ref, e1_ref, rb_ref, o_ref,
                       a0_ref, a1_ref, w0_ref, w1_ref, *, ib, n_eb):
    s = pl.program_id(0)

    @pl.when(s == 0)
    def _():
        for ref in (a0_ref, a1_ref, w0_ref, w1_ref):
            ref[...] = jnp.zeros_like(ref)

    @pl.when((s < 2) | ((s - 2) % n_eb == 0))
    def _():
        o_ref[...] = jnp.zeros_like(o_ref)

    io = (ht_ref, u_ref, vt_ref, e0_ref, li_ref, e1_ref, rb_ref, o_ref)

    @pl.when(s % 2 == 0)
    def _():
        _peer_dense_stages(*io, a0_ref, a1_ref, w1_ref, w0_ref, ib=ib)

    @pl.when(s % 2 == 1)
    def _():
        _peer_dense_stages(*io, a1_ref, a0_ref, w0_ref, w1_ref, ib=ib)


def peer_dense(h_t, u_bf16, vt_bf16, e0, li, e1, rb, *, tn, ib=8):
    d, ntok = h_t.shape
    rt = tn // LANES
    n_nb = ntok // tn
    ebs = ib * N_KEYS
    assert e1.shape == (PEER_HEADS, n_nb, N_KEYS * rt, LANES) and ntok == n_nb * tn
    assert N_KEYS % ib == 0 and ib % _DENSE_GROUP_KEYS1 == 0
    n_eb = N_KEYS // ib
    total = n_nb * n_eb

    def stage(lag):
        return lambda s: jnp.clip(s - lag, 0, total - 1)

    fa, fb, fc = stage(0), stage(1), stage(2)
    return pl.pallas_call(
        functools.partial(_peer_dense_kernel, ib=ib, n_eb=n_eb),
        grid=(total + 2,),
        in_specs=[pl.BlockSpec((d, tn), lambda s: (0, fa(s) // n_eb)),
                  pl.BlockSpec((ebs, d), lambda s: (fa(s) % n_eb, 0)),
                  pl.BlockSpec((d, ebs), lambda s: (0, fc(s) % n_eb)),
                  pl.BlockSpec((PEER_HEADS, 1, ib * rt, LANES),
                               lambda s: (0, fb(s) // n_eb, fb(s) % n_eb, 0)),
                  pl.BlockSpec((PEER_HEADS, 1, ib * rt, LANES),
                               lambda s: (0, fb(s) // n_eb, fb(s) % n_eb, 0)),
                  pl.BlockSpec((PEER_HEADS, 1, N_KEYS * rt, LANES),
                               lambda s: (0, fb(s) // n_eb, 0, 0)),
                  pl.BlockSpec((PEER_HEADS, 1, N_KEYS * rt, LANES),
                               lambda s: (0, fb(s) // n_eb, 0, 0))],
        out_specs=pl.BlockSpec((d, tn), lambda s: (0, fc(s) // n_eb)),
        out_shape=jax.ShapeDtypeStruct((d, ntok), F32),
        scratch_shapes=[pltpu.VMEM((ebs, tn), F32), pltpu.VMEM((ebs, tn), F32),
                        pltpu.VMEM((ebs, tn), BF16), pltpu.VMEM((ebs, tn), BF16)],
        compiler_params=_params("arbitrary"),
        name="peer_dense",
    )(h_t, u_bf16, vt_bf16, e0, li, e1, rb)


def _table_kernel(x_ref, o_ref, *, transposed):
    x = x_ref[...]
    o_ref[...] = (jnp.transpose(x) if transposed else x).astype(o_ref.dtype)


def expert_table_bf16(tabs, layer, *, transposed, rows=512):
    _, e, d = tabs.shape
    assert e % rows == 0
    if transposed:
        out_shape, out_spec = (d, e), pl.BlockSpec((d, rows), lambda i: (0, i))
    else:
        out_shape, out_spec = (e, d), pl.BlockSpec((rows, d), lambda i: (i, 0))
    return pl.pallas_call(
        functools.partial(_table_kernel, transposed=transposed),
        grid=(e // rows,),
        in_specs=[pl.BlockSpec((None, rows, d), lambda i: (layer, i, 0))],
        out_specs=out_spec,
        out_shape=jax.ShapeDtypeStruct(out_shape, BF16),
        compiler_params=_params("parallel"),
        name="expert_table",
    )(tabs)


def _add_transposed_kernel(x_ref, yt_ref, o_ref):
    o_ref[...] = x_ref[...] + jnp.transpose(yt_ref[...])


def add_transposed(x, y_t, *, tm):
    n, d = x.shape
    return pl.pallas_call(
        _add_transposed_kernel,
        grid=(n // tm,),
        in_specs=[pl.BlockSpec((tm, d), lambda i: (i, 0)), pl.BlockSpec((d, tm), lambda i: (0, i))],
        out_specs=pl.BlockSpec((tm, d), lambda i: (i, 0)),
        out_shape=jax.ShapeDtypeStruct((n, d), F32),
        compiler_params=_params("parallel"),
        name="add_transposed",
    )(x, y_t)


_TM = (2048, DEC_BATCH)
_TB = (512, DEC_BATCH)


def peer_layer(xs, norm_g, w_q, keys, u_tabs, v_tabs, layer):
    w_q = w_q.astype(BF16)
    keys_bf16 = keys.reshape(2 * PEER_HEADS, N_KEYS, PEER_HALF).astype(BF16)
    u_bf16 = expert_table_bf16(u_tabs, layer, transposed=False)
    vt_bf16 = expert_table_bf16(v_tabs, layer, transposed=True)
    out = []
    for x, tm, tb in zip(xs, _TM, _TB):
        h, h_t = rmsnorm(x, norm_g, BF16, tm=tb, transposed=True)
        q = matmul([h], [w_q], tm=tm, tn=512)
        rows = x.shape[0] // LANES
        sel = peer_select(q, keys_bf16, rows=rows, rblk=min(rows, 32), rt=tb // LANES)
        y_t = peer_dense(h_t, u_bf16, vt_bf16, *sel, tn=tb)
        out.append(add_transposed(x, y_t, tm=tb))
    return out


def kernel(x_prompt, x_sample, cache_mem_k, cache_mem_v, state_pool, state_gla, mem_prompt,
           norm_mix, norm_ffn, norm_final, mem_norm, w_mem_kv,
           pool_w_in, pool_w_group, pool_scale, pool_w_out,
           gla_w_in, gla_w_gate, gla_b_gate, gla_norm, gla_w_out,
           peer_w_q, peer_keys, peer_u, peer_v):
    d = D_MODEL
    xs = [x_prompt.reshape(N_PROMPT, d), x_sample.reshape(DEC_BATCH, d)]

    def project(xs, norm_g, w):
        return [matmul([rmsnorm(x, norm_g, BF16, tm=tb)], [w], tm=tm, tn=512)
                for x, tm, tb in zip(xs, _TM, _TB)]

    def mix_out(xs, mixes, atts, w, split):
        return [matmul([m, a], [w[:split], w[split:]], x, tm=tm // 2, tn=512)
                for x, m, a, tm in zip(xs, mixes, atts, _TM)]

    mem = mem_prompt.reshape(BATCH * MEM_TOKENS, d)
    kvs = []
    for i in range(DEPTH):
        mn = rmsnorm(mem, mem_norm[i], BF16, tm=512)
        kvs.append(matmul([mn], [w_mem_kv[i].astype(BF16)], tm=512, tn=512))

    def split_kv(kv, lo):
        return kv[:, lo:lo + MEM_WIDTH].reshape(BATCH, MEM_TOKENS, MEM_HEADS, MEM_HEAD_DIM)

    mem_k_prompt = jnp.stack([split_kv(kv, 0) for kv in kvs])
    mem_v_prompt = jnp.stack([split_kv(kv, MEM_WIDTH) for kv in kvs])

    u_p, u_s = project(xs, norm_mix[0], pool_w_in[0].astype(BF16))
    p_s = u_s[:, :POOL_WIDTH]
    mixes = [pool_prompt(u_p, pool_w_group[0], pool_scale[0]),
             pool_sample(jnp.transpose(state_pool[0], (1, 0, 2)), p_s, pool_w_group[0], pool_scale[0])]
    atts = [attn_prompt(u_p, POOL_WIDTH, kvs[0]),
            attn_sample(u_s[:, POOL_WIDTH:], cache_mem_k, cache_mem_v, 0)]
    xs = mix_out(xs, mixes, atts, pool_w_out[0].astype(BF16), POOL_WIDTH)
    pool_prompt_state = u_p[:, :POOL_WIDTH].reshape(BATCH, SEQ, POOL_WIDTH)[:, SEQ - POOL_BUF:][None]
    pool_sample_state = jnp.concatenate([state_pool[0][:, 1:], p_s[:, None, :]], axis=1)[None]
    xs = peer_layer(xs, norm_ffn[0], peer_w_q[0], peer_keys[0], peer_u, peer_v, 0)

    w_in = gla_w_in[0]
    o3 = 2 * GLA_KEY + GLA_VAL
    o4 = o3 + GATE_RANK
    o5 = o4 + GLA_VAL
    w_in = jnp.concatenate([
        w_in[:, :o3], w_in[:, o4:o5], w_in[:, o5:], w_in[:, o3:o4],
        jnp.zeros((d, GLA_Z_PAD - GATE_RANK), w_in.dtype)], axis=1).astype(BF16)
    wg_pad = jnp.concatenate(
        [gla_w_gate[0], jnp.zeros((LANES - GATE_RANK, GLA_KEY), gla_w_gate.dtype)], axis=0).astype(BF16)
    u_p, u_s = project(xs, norm_mix[1], w_in)
    mix_p, st_t = gla_prompt(u_p, wg_pad, gla_b_gate[0], gla_norm[0])
    mix_s, gla_s = gla_sample(u_s, state_gla[0], wg_pad, gla_b_gate[0], gla_norm[0])
    atts = [attn_prompt(u_p, GLA_COL_QM, kvs[1]),
            attn_sample(u_s[:, GLA_COL_QM:GLA_COL_QM + MEM_WIDTH], cache_mem_k, cache_mem_v, 1)]
    xs = mix_out(xs, [mix_p, mix_s.astype(BF16)], atts, gla_w_out[0].astype(BF16), GLA_VAL)
    xs = peer_layer(xs, norm_ffn[1], peer_w_q[1], peer_keys[1], peer_u, peer_v, 1)

    y_prompt, y_sample = [rmsnorm(x, norm_final, F32, tm=tb) for x, tb in zip(xs, _TB)]
    return (y_prompt.reshape(BATCH, SEQ, d),
            y_sample.reshape(DEC_BATCH, 1, d),
            pool_prompt_state,
            jnp.swapaxes(st_t, -1, -2)[None],
            mem_k_prompt,
            mem_v_prompt,
            pool_sample_state,
            gla_s[None])
```
